```python
import math
import jax, jax.numpy as jnp
from jax import lax
import numpy as np

D_MODEL = 1024
BATCH = 2
SEQ = 16384
DEPTH = 4

CHUNK = 128
N_SGU_GROUPS = 8
SGU_WIDTH = D_MODEL
SGU_GROUP = SGU_WIDTH // N_SGU_GROUPS
DIFF_HEAD_DIM = 64
N_DIFF_HEADS = D_MODEL // (2 * DIFF_HEAD_DIM)
DIFF_QK_WIDTH = N_DIFF_HEADS * 2 * DIFF_HEAD_DIM
DIFF_V_WIDTH = N_DIFF_HEADS * 2 * DIFF_HEAD_DIM
Q_BLOCK = 128
ROPE_THETA = 10000.0
N_BRANCH = 2
IN_WIDTH = 2 * SGU_WIDTH + 2 * DIFF_QK_WIDTH + DIFF_V_WIDTH + N_BRANCH * D_MODEL
SPLITS = [SGU_WIDTH, 2 * SGU_WIDTH, 2 * SGU_WIDTH + DIFF_QK_WIDTH,
          2 * SGU_WIDTH + 2 * DIFF_QK_WIDTH,
          2 * SGU_WIDTH + 2 * DIFF_QK_WIDTH + DIFF_V_WIDTH]
D_FF = 2816
ALPHA = (2.0 * DEPTH) ** 0.25
BETA = (8.0 * DEPTH) ** -0.25
LN_EPS = 1e-5

kernel_name = "hybrid_sgu_diffattn_macaron_deepnorm"


def _lambda_init(layer):
    return 0.8 - 0.6 * math.exp(-0.3 * layer)


def _layernorm(x, g, b):
    xf = x.astype(jnp.float32)
    mu = jnp.mean(xf, axis=-1, keepdims=True)
    var = jnp.mean(jnp.square(xf - mu), axis=-1, keepdims=True)
    y = (xf - mu) * lax.rsqrt(var + LN_EPS)
    return (y * g.astype(jnp.float32) + b.astype(jnp.float32)).astype(x.dtype)


def _rmsnorm(x, g):
    xf = x.astype(jnp.float32)
    y = xf * lax.rsqrt(jnp.mean(jnp.square(xf), axis=-1, keepdims=True) + LN_EPS)
    return (y * g.astype(jnp.float32)).astype(x.dtype)


def _swiglu(x, w1, w3, w2):
    return (jax.nn.silu(x @ w1) * (x @ w3)) @ w2


def _rope(x, cos, sin):
    c = cos[None, :, None, None, :]
    s = sin[None, :, None, None, :]
    xf = x.astype(jnp.float32)
    x1, x2 = jnp.split(xf, 2, axis=-1)
    out = jnp.concatenate([x1 * c - x2 * s, x2 * c + x1 * s], axis=-1)
    return out.astype(x.dtype)


def _chunked_sgu(u, v, g, b, w_s, b_s):
    bsz, seq, _ = v.shape
    v = _layernorm(v, g, b)
    n_chunks = seq // CHUNK
    vb = v.reshape(bsz, n_chunks, CHUNK, N_SGU_GROUPS, SGU_GROUP)
    causal = jnp.tril(jnp.ones((CHUNK, CHUNK), dtype=bool))
    w = jnp.where(causal[None], w_s, jnp.zeros((), w_s.dtype))
    s = jnp.einsum('gtr,bnrgc->bntgc', w, vb) + b_s.T[None, None, :, :, None]
    return u * s.reshape(bsz, seq, SGU_WIDTH).astype(u.dtype)


def _diff_attention(q, k, v, lam):
    bsz, seq = q.shape[:2]
    n_blocks = seq // Q_BLOCK
    qb = q.reshape(bsz, n_blocks, Q_BLOCK, N_DIFF_HEADS, 2, DIFF_HEAD_DIM).swapaxes(0, 1)
    k_pos = jnp.arange(seq)
    scale = DIFF_HEAD_DIM ** -0.5
    neg = jnp.finfo(jnp.float32).min

    def one_block(args):
        q_blk, i = args
        s = jnp.einsum('bqhmd,bkhmd->bhmqk', q_blk, k).astype(jnp.float32) * scale
        q_pos = i * Q_BLOCK + jnp.arange(Q_BLOCK)
        mask = k_pos[None, :] <= q_pos[:, None]
        s = jnp.where(mask, s, neg)
        p = jax.nn.softmax(s, axis=-1)
        a = p[:, :, 0] - lam * p[:, :, 1]
        return jnp.einsum('bhqk,bkhe->bqhe', a.astype(v.dtype), v)

    out = lax.map(one_block, (qb, jnp.arange(n_blocks)))
    return out.swapaxes(0, 1).reshape(bsz, seq, N_DIFF_HEADS, 2 * DIFF_HEAD_DIM)


def _mixer(x, layer, w_in, gate_b, sgu_ln_g, sgu_ln_b, sgu_w, sgu_b, lam, diff_ln_g,
           w_branch, w_out, cos, sin):
    bsz, seq, _ = x.shape
    z = x @ w_in
    u, v, q, k, val, gates = jnp.split(z, SPLITS, axis=-1)
    u = jax.nn.gelu(u, approximate=False)
    v = jax.nn.gelu(v, approximate=False)
    h_a = _chunked_sgu(u, v, sgu_ln_g, sgu_ln_b, sgu_w, sgu_b)
    q = _rope(q.reshape(bsz, seq, N_DIFF_HEADS, 2, DIFF_HEAD_DIM), cos, sin)
    k = _rope(k.reshape(bsz, seq, N_DIFF_HEADS, 2, DIFF_HEAD_DIM), cos, sin)
    val = val.reshape(bsz, seq, N_DIFF_HEADS, 2 * DIFF_HEAD_DIM)
    lam_init = _lambda_init(layer)
    lf = lam.astype(jnp.float32)
    lam_full = (jnp.exp(jnp.sum(lf[0] * lf[1])) - jnp.exp(jnp.sum(lf[2] * lf[3]))
                + lam_init)
    o = _diff_attention(q, k, val, lam_full)
    o = _rmsnorm(o, diff_ln_g) * (1.0 - lam_init)
    h_b = o.reshape(bsz, seq, DIFF_V_WIDTH)
    g = jax.nn.sigmoid(gates.reshape(bsz, seq, N_BRANCH, D_MODEL) + gate_b)
    m = g[:, :, 0] * (h_a @ w_branch[0]) + g[:, :, 1] * (h_b @ w_branch[1])
    return m @ w_out


def setup_inputs(seed: int = 0) -> dict:
    key = jax.random.key(seed)
    ks = jax.random.split(key, 20)
    f32 = jnp.float32
    nrm = lambda k, shape, s: jax.random.normal(k, shape, f32) * s
    x = jax.random.normal(ks[0], (BATCH, SEQ, D_MODEL), f32)
    w_in = nrm(ks[1], (DEPTH, D_MODEL, IN_WIDTH), D_MODEL ** -0.5)
    gate_b = nrm(ks[2], (DEPTH, N_BRANCH, D_MODEL), 0.02)
    sgu_ln_g = 1.0 + nrm(ks[3], (DEPTH, SGU_WIDTH), 0.02)
    sgu_ln_b = nrm(ks[4], (DEPTH, SGU_WIDTH), 0.02)
    sgu_w = nrm(ks[5], (DEPTH, N_SGU_GROUPS, CHUNK, CHUNK), CHUNK ** -0.5)
    sgu_b = 1.0 + nrm(ks[6], (DEPTH, N_SGU_GROUPS, CHUNK), 0.02)
    lam = nrm(ks[7], (DEPTH, 4, DIFF_HEAD_DIM), 0.1)
    diff_ln_g = 1.0 + nrm(ks[8], (DEPTH, 2 * DIFF_HEAD_DIM), 0.02)
    w_branch = jnp.concatenate([
        nrm(ks[9], (DEPTH, 1, SGU_WIDTH, D_MODEL), BETA * SGU_WIDTH ** -0.5),
        nrm(ks[10], (DEPTH, 1, DIFF_V_WIDTH, D_MODEL), BETA * DIFF_V_WIDTH ** -0.5)], axis=1)
    w_out = nrm(ks[11], (DEPTH, D_MODEL, D_MODEL), BETA * D_MODEL ** -0.5)
    ffn_w1 = nrm(ks[12], (DEPTH, 2, D_MODEL, D_FF), D_MODEL ** -0.5)
    ffn_w3 = nrm(ks[13], (DEPTH, 2, D_MODEL, D_FF), D_MODEL ** -0.5)
    ffn_w2 = nrm(ks[14], (DEPTH, 2, D_FF, D_MODEL), BETA * D_FF ** -0.5)
    ln_g = 1.0 + nrm(ks[15], (DEPTH, 3, D_MODEL), 0.02)
    ln_b = nrm(ks[16], (DEPTH, 3, D_MODEL), 0.02)
    return {"x": x, "w_in": w_in, "gate_b": gate_b, "sgu_ln_g": sgu_ln_g,
            "sgu_ln_b": sgu_ln_b, "sgu_w": sgu_w, "sgu_b": sgu_b, "lam": lam,
            "diff_ln_g": diff_ln_g, "w_branch": w_branch, "w_out": w_out,
            "ffn_w1": ffn_w1, "ffn_w3": ffn_w3, "ffn_w2": ffn_w2,
            "ln_g": ln_g, "ln_b": ln_b}


def reference(x, w_in, gate_b, sgu_ln_g, sgu_ln_b, sgu_w, sgu_b, lam, diff_ln_g,
              w_branch, w_out, ffn_w1, ffn_w3, ffn_w2, ln_g, ln_b):
    seq = x.shape[1]
    pos = jnp.arange(seq, dtype=jnp.float32)
    inv_freq = ROPE_THETA ** (-jnp.arange(0, DIFF_HEAD_DIM, 2, dtype=jnp.float32) / DIFF_HEAD_DIM)
    ang = pos[:, None] * inv_freq[None, :]
    cos, sin = jnp.cos(ang), jnp.sin(ang)
    for l in range(DEPTH):
        h = _swiglu(x, ffn_w1[l, 0], ffn_w3[l, 0], ffn_w2[l, 0])
        x = _layernorm(ALPHA * x + 0.5 * h, ln_g[l, 0], ln_b[l, 0])
        h = _mixer(x, l, w_in[l], gate_b[l], sgu_ln_g[l], sgu_ln_b[l], sgu_w[l], sgu_b[l],
                   lam[l], diff_ln_g[l], w_branch[l], w_out[l], cos, sin)
        x = _layernorm(ALPHA * x + h, ln_g[l, 1], ln_b[l, 1])
        h = _swiglu(x, ffn_w1[l, 1], ffn_w3[l, 1], ffn_w2[l, 1])
        x = _layernorm(ALPHA * x + 0.5 * h, ln_g[l, 2], ln_b[l, 2])
    return x
```

```python
import functools
import math

import jax
import jax.numpy as jnp
from jax import lax
from jax.experimental import pallas as pl
from jax.experimental.pallas import tpu as pltpu

D_MODEL = 1024
DEPTH = 4
CHUNK = 128
N_SGU_GROUPS = 8
SGU_GROUP = D_MODEL // N_SGU_GROUPS
HEAD_DIM = 64
N_HEADS = D_MODEL // (2 * HEAD_DIM)
HEAD_WIDTH = 2 * HEAD_DIM
D_FF = 2816
ROPE_THETA = 10000.0
ALPHA = (2.0 * DEPTH) ** 0.25
LN_EPS = 1e-5

OFF_U, OFF_V, OFF_Q, OFF_K, OFF_VAL, OFF_GA, OFF_GB = (
    0, D_MODEL, 2 * D_MODEL, 3 * D_MODEL, 4 * D_MODEL, 5 * D_MODEL, 6 * D_MODEL)
IN_WIDTH = 7 * D_MODEL

VMEM_LIMIT_BYTES = 56 * 1024 * 1024
LANES = 128
MASK_VALUE = -1e30

FFN_TILE = 512
MIXER_TILE = 512
ATTN_TILE = 512
FFN_CHUNKS = ((0, 1280), (1280, D_FF))

_BF16 = jnp.bfloat16
_F32 = jnp.float32


def _lambda_init(layer):
    return 0.8 - 0.6 * math.exp(-0.3 * layer)


def _layernorm(y, g, b):
    mu = jnp.mean(y, axis=-1, keepdims=True)
    d = y - mu
    var = jnp.mean(d * d, axis=-1, keepdims=True)
    return d * lax.rsqrt(var + LN_EPS) * g + b


def _gelu(x):
    return 0.5 * x * (1.0 + lax.erf(x * (2.0 ** -0.5)))


def _dot(a, b):
    return jnp.dot(a, b, preferred_element_type=_F32)


def _resident(shape):
    return pl.BlockSpec(shape, lambda *_: (0,) * len(shape), pipeline_mode=pl.Buffered(1))


def _params(*semantics):
    return pltpu.CompilerParams(dimension_semantics=semantics, vmem_limit_bytes=VMEM_LIMIT_BYTES)


def _ffn_ln_kernel(x_ref, w1_ref, w3_ref, w2_ref, g_ref, b_ref, o_ref):
    x = x_ref[...]
    xb = x.astype(_BF16)
    h = None
    for lo, hi in FFN_CHUNKS:
        gate = _dot(xb, w1_ref[:, lo:hi])
        up = _dot(xb, w3_ref[:, lo:hi])
        act = (gate * jax.nn.sigmoid(gate) * up).astype(_BF16)
        part = _dot(act, w2_ref[lo:hi, :])
        h = part if h is None else h + part
    o_ref[...] = _layernorm(ALPHA * x + 0.5 * h, g_ref[...], b_ref[...])


def _ffn_ln(x, w1, w3, w2, g, b):
    t = x.shape[0]
    tm = min(FFN_TILE, t)
    row = pl.BlockSpec((tm, D_MODEL), lambda i: (i, 0))
    return pl.pallas_call(
        _ffn_ln_kernel,
        grid=(t // tm,),
        in_specs=[row, _resident((D_MODEL, D_FF)), _resident((D_MODEL, D_FF)),
                  _resident((D_FF, D_MODEL)), _resident((1, D_MODEL)), _resident((1, D_MODEL))],
        out_specs=row,
        out_shape=jax.ShapeDtypeStruct((t, D_MODEL), _F32),
        compiler_params=_params("parallel"),
        name="ffn_ln",
    )(x, w1, w3, w2, g, b)


def _mixer_in_kernel(x_ref, w_ref, gate_b_ref, sg_ref, sb_ref, sw_ref, sbias_ref, wa_ref,
                     cos_ref, sin_a_ref, sin_b_ref,
                     q_ref, k_ref, v_ref, ma_ref, gb_ref, ha_ref):
    tm = x_ref.shape[0]
    xb = x_ref[...].astype(_BF16)

    def proj(off, width=D_MODEL):
        return _dot(xb, w_ref[:, off:off + width])

    u = _gelu(proj(OFF_U))
    v = _layernorm(_gelu(proj(OFF_V)), sg_ref[...], sb_ref[...]).astype(_BF16)
    row = lax.broadcasted_iota(jnp.int32, (CHUNK, CHUNK), 0)
    col = lax.broadcasted_iota(jnp.int32, (CHUNK, CHUNK), 1)
    causal = col <= row
    for g in range(N_SGU_GROUPS):
        w_g = jnp.where(causal, sw_ref[g], 0.0).astype(_BF16)
        bias_g = sbias_ref[g]
        cols = slice(g * SGU_GROUP, (g + 1) * SGU_GROUP)
        for c in range(tm // CHUNK):
            rows = slice(c * CHUNK, (c + 1) * CHUNK)
            s = _dot(w_g, v[rows, cols]) + bias_g
            ha_ref[rows, cols] = (u[rows, cols] * s).astype(_BF16)
    gate_a = jax.nn.sigmoid(proj(OFF_GA) + gate_b_ref[0:1, :])
    ma_ref[...] = gate_a * _dot(ha_ref[...], wa_ref[...])
    gb_ref[...] = jax.nn.sigmoid(proj(OFF_GB) + gate_b_ref[1:2, :])

    cos, sin_a, sin_b = cos_ref[...], sin_a_ref[...], sin_b_ref[...]

    def rope(z):
        return (z * cos + pltpu.roll(z, LANES - HEAD_DIM // 2, axis=1) * sin_a
                + pltpu.roll(z, HEAD_DIM // 2, axis=1) * sin_b)

    zq = proj(OFF_Q)
    zk = proj(OFF_K)
    zv = proj(OFF_VAL)
    scale = HEAD_DIM ** -0.5
    for h in range(N_HEADS):
        cols = slice(h * HEAD_WIDTH, (h + 1) * HEAD_WIDTH)
        q_ref[h] = (rope(zq[:, cols]) * scale).astype(_BF16)
        k_ref[h] = rope(zk[:, cols]).astype(_BF16)
        v_ref[h] = zv[:, cols].astype(_BF16)


def _mixer_in(x, w_in, gate_b, sgu_g, sgu_b, sgu_w, sgu_bias, w_a, cos, sin_a, sin_b, bsz, seq):
    t = x.shape[0]
    tm = min(MIXER_TILE, seq)
    per_seq = seq // tm
    row = pl.BlockSpec((tm, D_MODEL), lambda i: (i, 0))
    pos = pl.BlockSpec((tm, HEAD_WIDTH), lambda i: (i % per_seq, 0))
    head = pl.BlockSpec((None, N_HEADS, tm, HEAD_WIDTH), lambda i: (i // per_seq, 0, i % per_seq, 0))
    head_shape = jax.ShapeDtypeStruct((bsz, N_HEADS, seq, HEAD_WIDTH), _BF16)
    return pl.pallas_call(
        _mixer_in_kernel,
        grid=(t // tm,),
        in_specs=[row, _resident((D_MODEL, IN_WIDTH)), _resident((2, D_MODEL)),
                  _resident((1, D_MODEL)), _resident((1, D_MODEL)),
                  _resident((N_SGU_GROUPS, CHUNK, CHUNK)), _resident((N_SGU_GROUPS, CHUNK, 1)),
                  _resident((D_MODEL, D_MODEL)), pos, pos, pos],
        out_specs=[head, head, head, row, row],
        out_shape=[head_shape, head_shape, head_shape,
                   jax.ShapeDtypeStruct((t, D_MODEL), _F32), jax.ShapeDtypeStruct((t, D_MODEL), _F32)],
        scratch_shapes=[pltpu.VMEM((tm, D_MODEL), _BF16)],
        compiler_params=_params("parallel"),
        name="mixer_in",
    )(x, w_in, gate_b, sgu_g, sgu_b, sgu_w, sgu_bias, w_a, cos, sin_a, sin_b)


def _diff_attn_kernel(lam_ref, g_ref, q_ref, k_ref, v_ref, o_ref,
                      acc1, acc2, m1, m2, l1, l2, *, lam_init):
    tq = q_ref.shape[0]
    qi = pl.program_id(2)
    q = q_ref[...]
    lane = lax.broadcasted_iota(jnp.int32, q.shape, 1)
    zero = jnp.zeros_like(q)
    q_maps = (jnp.where(lane < HEAD_DIM, q, zero), jnp.where(lane >= HEAD_DIM, q, zero))
    state = ((acc1, m1, l1), (acc2, m2, l2))
    for acc, m, l in state:
        acc[...] = jnp.zeros_like(acc)
        m[...] = jnp.full_like(m, MASK_VALUE)
        l[...] = jnp.zeros_like(l)

    def step(j, masked):
        start = pl.multiple_of(j * tq, tq)
        kb = k_ref[pl.ds(start, tq), :]
        vb = v_ref[pl.ds(start, tq), :]
        for qm, (acc, m, l) in zip(q_maps, state):
            s = lax.dot_general(qm, kb, (((1,), (1,)), ((), ())), preferred_element_type=_F32)
            if masked:
                r = lax.broadcasted_iota(jnp.int32, s.shape, 0)
                c = lax.broadcasted_iota(jnp.int32, s.shape, 1)
                s = jnp.where(c <= r, s, MASK_VALUE)
            m_old = m[...]
            m_new = jnp.maximum(m_old, jnp.max(s, axis=-1, keepdims=True))
            alpha = jnp.exp(m_old - m_new)
            p = jnp.exp(s - m_new)
            l[...] = alpha * l[...] + jnp.sum(p, axis=-1, keepdims=True)
            acc[...] = alpha * acc[...] + _dot(p.astype(_BF16), vb)
            m[...] = m_new

    def body(j, carry):
        step(j, masked=False)
        return carry

    lax.fori_loop(0, qi, body, 0)
    step(qi, masked=True)

    lam = lam_ref[...]
    lam_full = (jnp.exp(jnp.sum(lam[0:1] * lam[1:2], keepdims=True))
                - jnp.exp(jnp.sum(lam[2:3] * lam[3:4], keepdims=True)) + lam_init)
    o = acc1[...] / l1[...] - lam_full * (acc2[...] / l2[...])
    o = o * lax.rsqrt(jnp.mean(o * o, axis=-1, keepdims=True) + LN_EPS) * g_ref[...]
    o_ref[...] = (o * (1.0 - lam_init)).astype(o_ref.dtype)


def _diff_attn(q, k, v, lam, g, lam_init):
    bsz, _, seq, _ = q.shape
    tq = min(ATTN_TILE, seq)
    q_spec = pl.BlockSpec((None, None, tq, HEAD_WIDTH), lambda b, h, i: (b, h, i, 0))
    kv_spec = pl.BlockSpec((None, None, seq, HEAD_WIDTH), lambda b, h, i: (b, h, 0, 0))
    stat = pltpu.VMEM((tq, 1), _F32)
    acc = pltpu.VMEM((tq, HEAD_WIDTH), _F32)
    return pl.pallas_call(
        functools.partial(_diff_attn_kernel, lam_init=lam_init),
        grid=(bsz, N_HEADS, seq // tq),
        in_specs=[_resident((4, HEAD_DIM)), _resident((1, HEAD_WIDTH)), q_spec, kv_spec, kv_spec],
        out_specs=q_spec,
        out_shape=jax.ShapeDtypeStruct(q.shape, _BF16),
        scratch_shapes=[acc, acc, stat, stat, stat, stat],
        compiler_params=_params("parallel", "parallel", "arbitrary"),
        name="diff_attn",
    )(lam, g, q, k, v)


def _mixer_out_kernel(x_ref, hb_ref, ma_ref, gb_ref, wb_ref, wo_ref, g_ref, b_ref, o_ref):
    hb = jnp.concatenate([hb_ref[h] for h in range(N_HEADS)], axis=-1)
    m = ma_ref[...] + gb_ref[...] * _dot(hb, wb_ref[...])
    y = _dot(m.astype(_BF16), wo_ref[...])
    o_ref[...] = _layernorm(ALPHA * x_ref[...] + y, g_ref[...], b_ref[...])


def _mixer_out(x, hb, ma, gb, w_b, w_out, g, b, seq):
    t = x.shape[0]
    tm = min(MIXER_TILE, seq)
    per_seq = seq // tm
    row = pl.BlockSpec((tm, D_MODEL), lambda i: (i, 0))
    head = pl.BlockSpec((None, N_HEADS, tm, HEAD_WIDTH), lambda i: (i // per_seq, 0, i % per_seq, 0))
    return pl.pallas_call(
        _mixer_out_kernel,
        grid=(t // tm,),
        in_specs=[row, head, row, row, _resident((D_MODEL, D_MODEL)), _resident((D_MODEL, D_MODEL)),
                  _resident((1, D_MODEL)), _resident((1, D_MODEL))],
        out_specs=row,
        out_shape=jax.ShapeDtypeStruct((t, D_MODEL), _F32),
        compiler_params=_params("parallel"),
        name="mixer_out",
    )(x, hb, ma, gb, w_b, w_out, g, b)


def _rope_tables(seq):
    half = HEAD_DIM // 2
    inv_freq = ROPE_THETA ** (-jnp.arange(0, HEAD_DIM, 2, dtype=_F32) / HEAD_DIM)
    ang = jnp.arange(seq, dtype=_F32)[:, None] * inv_freq[None, :]
    cos, sin = jnp.cos(ang), jnp.sin(ang)
    zeros = jnp.zeros_like(sin)
    reps = HEAD_WIDTH // HEAD_DIM
    cos_t = jnp.tile(jnp.concatenate([cos, cos], axis=-1), (1, reps))
    sin_a = jnp.tile(jnp.concatenate([-sin, zeros], axis=-1), (1, reps))
    sin_b = jnp.tile(jnp.concatenate([zeros, sin], axis=-1), (1, reps))
    del half
    return cos_t, sin_a, sin_b


def kernel(x, w_in, gate_b, sgu_ln_g, sgu_ln_b, sgu_w, sgu_b, lam, diff_ln_g, w_branch, w_out,
           ffn_w1, ffn_w3, ffn_w2, ln_g, ln_b):
    bsz, seq, _ = x.shape
    cos, sin_a, sin_b = _rope_tables(seq)
    h = x.reshape(bsz * seq, D_MODEL)
    bf = lambda w: w.astype(_BF16)
    for l in range(DEPTH):
        vec = lambda a: a.reshape(1, -1)
        h = _ffn_ln(h, bf(ffn_w1[l, 0]), bf(ffn_w3[l, 0]), bf(ffn_w2[l, 0]),
                    vec(ln_g[l, 0]), vec(ln_b[l, 0]))
        q, k, v, ma, gb = _mixer_in(
            h, bf(w_in[l]), gate_b[l], vec(sgu_ln_g[l]), vec(sgu_ln_b[l]), sgu_w[l],
            sgu_b[l].reshape(N_SGU_GROUPS, CHUNK, 1), bf(w_branch[l, 0]), cos, sin_a, sin_b, bsz, seq)
        hb = _diff_attn(q, k, v, lam[l], vec(diff_ln_g[l]), _lambda_init(l))
        h = _mixer_out(h, hb, ma, gb, bf(w_branch[l, 1]), bf(w_out[l]),
                       vec(ln_g[l, 1]), vec(ln_b[l, 1]), seq)
        h = _ffn_ln(h, bf(ffn_w1[l, 1]), bf(ffn_w3[l, 1]), bf(ffn_w2[l, 1]),
                    vec(ln_g[l, 2]), vec(ln_b[l, 2]))
    return h.reshape(bsz, seq, D_MODEL)
```

```python
import functools
import math

import jax
import jax.numpy as jnp
from jax import lax
from jax.experimental import pallas as pl
from jax.experimental.pallas import tpu as pltpu

D_MODEL = 1024
DEPTH = 4
CHUNK = 128
N_SGU_GROUPS = 8
SGU_GROUP = D_MODEL // N_SGU_GROUPS
HEAD_DIM = 64
N_HEADS = D_MODEL // (2 * HEAD_DIM)
HEAD_WIDTH = 2 * HEAD_DIM
D_FF = 2816
ROPE_THETA = 10000.0
ALPHA = (2.0 * DEPTH) ** 0.25
LN_EPS = 1e-5

OFF_U, OFF_V, OFF_Q, OFF_K, OFF_VAL, OFF_GA, OFF_GB = (
    0, D_MODEL, 2 * D_MODEL, 3 * D_MODEL, 4 * D_MODEL, 5 * D_MODEL, 6 * D_MODEL)
IN_WIDTH = 7 * D_MODEL

VMEM_LIMIT_BYTES = 56 * 1024 * 1024
LANES = 128
BF16_SUBLANES = 16
MASK_VALUE = -1e30
ONES_ROWS = BF16_SUBLANES
V_ROWS = HEAD_WIDTH + ONES_ROWS

FFN_TILE = 512
MIXER_TILE = 512
ATTN_TILE = 512
FFN_CHUNKS = ((0, 1280), (1280, D_FF))

_BF16 = jnp.bfloat16
_F32 = jnp.float32


def _lambda_init(layer):
    return 0.8 - 0.6 * math.exp(-0.3 * layer)


def _layernorm(y, g, b):
    mu = jnp.mean(y, axis=-1, keepdims=True)
    d = y - mu
    var = jnp.mean(d * d, axis=-1, keepdims=True)
    return d * lax.rsqrt(var + LN_EPS) * g + b


def _gelu(x):
    return 0.5 * x * (1.0 + lax.erf(x * (2.0 ** -0.5)))


def _dot(a, b):
    return jnp.dot(a, b, preferred_element_type=_F32)


def _resident(shape):
    return pl.BlockSpec(shape, lambda *_: (0,) * len(shape), pipeline_mode=pl.Buffered(1))


def _params(*semantics):
    return pltpu.CompilerParams(dimension_semantics=semantics, vmem_limit_bytes=VMEM_LIMIT_BYTES)


def _ffn_ln_kernel(x_ref, w1_ref, w3_ref, w2_ref, g_ref, b_ref, o_ref):
    x = x_ref[...]
    xb = x.astype(_BF16)
    h = None
    for lo, hi in FFN_CHUNKS:
        gate = _dot(xb, w1_ref[:, lo:hi])
        up = _dot(xb, w3_ref[:, lo:hi])
        act = (gate * jax.nn.sigmoid(gate) * up).astype(_BF16)
        part = _dot(act, w2_ref[lo:hi, :])
        h = part if h is None else h + part
    o_ref[...] = _layernorm(ALPHA * x + 0.5 * h, g_ref[...], b_ref[...])


def _ffn_ln(x, w1, w3, w2, g, b):
    t = x.shape[0]
    tm = min(FFN_TILE, t)
    row = pl.BlockSpec((tm, D_MODEL), lambda i: (i, 0))
    return pl.pallas_call(
        _ffn_ln_kernel,
        grid=(t // tm,),
        in_specs=[row, _resident((D_MODEL, D_FF)), _resident((D_MODEL, D_FF)),
                  _resident((D_FF, D_MODEL)), _resident((1, D_MODEL)), _resident((1, D_MODEL))],
        out_specs=row,
        out_shape=jax.ShapeDtypeStruct((t, D_MODEL), _F32),
        compiler_params=_params("parallel"),
        name="ffn_ln",
    )(x, w1, w3, w2, g, b)


def _mixer_in_kernel(x_ref, w_ref, gate_b_ref, sg_ref, sb_ref, sw_ref, sbias_ref, wa_ref,
                     cos_ref, sin_a_ref, sin_b_ref,
                     q_ref, k_ref, v_ref, ma_ref, gb_ref, ha_ref):
    tm = x_ref.shape[0]
    xb = x_ref[...].astype(_BF16)

    def proj(off, width=D_MODEL):
        return _dot(xb, w_ref[:, off:off + width])

    u = _gelu(proj(OFF_U))
    v = _layernorm(_gelu(proj(OFF_V)), sg_ref[...], sb_ref[...]).astype(_BF16)
    row = lax.broadcasted_iota(jnp.int32, (CHUNK, CHUNK), 0)
    col = lax.broadcasted_iota(jnp.int32, (CHUNK, CHUNK), 1)
    causal = col <= row
    for g in range(N_SGU_GROUPS):
        w_g = jnp.where(causal, sw_ref[g], 0.0).astype(_BF16)
        bias_g = sbias_ref[g]
        cols = slice(g * SGU_GROUP, (g + 1) * SGU_GROUP)
        for c in range(tm // CHUNK):
            rows = slice(c * CHUNK, (c + 1) * CHUNK)
            s = _dot(w_g, v[rows, cols]) + bias_g
            ha_ref[rows, cols] = (u[rows, cols] * s).astype(_BF16)
    gate_a = jax.nn.sigmoid(proj(OFF_GA) + gate_b_ref[0:1, :])
    ma_ref[...] = gate_a * _dot(ha_ref[...], wa_ref[...])
    gb_ref[...] = jax.nn.sigmoid(proj(OFF_GB) + gate_b_ref[1:2, :])

    cos, sin_a, sin_b = cos_ref[...], sin_a_ref[...], sin_b_ref[...]

    def rope(z):
        return (z * cos + pltpu.roll(z, LANES - HEAD_DIM // 2, axis=1) * sin_a
                + pltpu.roll(z, HEAD_DIM // 2, axis=1) * sin_b)

    zq = proj(OFF_Q)
    zk = proj(OFF_K)
    zv = proj(OFF_VAL)
    scale = HEAD_DIM ** -0.5 * math.log2(math.e)
    ones = jnp.ones((ONES_ROWS, tm), _BF16)
    for h in range(N_HEADS):
        cols = slice(h * HEAD_WIDTH, (h + 1) * HEAD_WIDTH)
        q_ref[h] = (rope(zq[:, cols]) * scale).T.astype(_BF16)
        k_ref[h] = rope(zk[:, cols]).astype(_BF16)
        v_ref[h, :HEAD_WIDTH, :] = zv[:, cols].T.astype(_BF16)
        v_ref[h, HEAD_WIDTH:, :] = ones


def _mixer_in(x, w_in, gate_b, sgu_g, sgu_b, sgu_w, sgu_bias, w_a, cos, sin_a, sin_b, bsz, seq):
    t = x.shape[0]
    tm = min(MIXER_TILE, seq)
    per_seq = seq // tm
    row = pl.BlockSpec((tm, D_MODEL), lambda i: (i, 0))
    pos = pl.BlockSpec((tm, HEAD_WIDTH), lambda i: (i % per_seq, 0))
    head = pl.BlockSpec((None, N_HEADS, tm, HEAD_WIDTH), lambda i: (i // per_seq, 0, i % per_seq, 0))
    head_shape = jax.ShapeDtypeStruct((bsz, N_HEADS, seq, HEAD_WIDTH), _BF16)
    def head_t(rows):
        return (pl.BlockSpec((None, N_HEADS, None, rows, tm), lambda i: (i // per_seq, 0, i % per_seq, 0, 0)),
                jax.ShapeDtypeStruct((bsz, N_HEADS, per_seq, rows, tm), _BF16))

    (q_spec, q_shape), (v_spec, v_shape) = head_t(HEAD_WIDTH), head_t(V_ROWS)
    return pl.pallas_call(
        _mixer_in_kernel,
        grid=(t // tm,),
        in_specs=[row, _resident((D_MODEL, IN_WIDTH)), _resident((2, D_MODEL)),
                  _resident((1, D_MODEL)), _resident((1, D_MODEL)),
                  _resident((N_SGU_GROUPS, CHUNK, CHUNK)), _resident((N_SGU_GROUPS, CHUNK, 1)),
                  _resident((D_MODEL, D_MODEL)), pos, pos, pos],
        out_specs=[q_spec, head, v_spec, row, row],
        out_shape=[q_shape, head_shape, v_shape,
                   jax.ShapeDtypeStruct((t, D_MODEL), _F32), jax.ShapeDtypeStruct((t, D_MODEL), _F32)],
        scratch_shapes=[pltpu.VMEM((tm, D_MODEL), _BF16)],
        compiler_params=_params("parallel"),
        name="mixer_in",
    )(x, w_in, gate_b, sgu_g, sgu_b, sgu_w, sgu_bias, w_a, cos, sin_a, sin_b)


def _diff_attn_kernel(lam_ref, g_ref, q_ref, k_ref, v_ref, o_ref,
                      s1a, s1b, s2a, s2b, acc1, acc2, m1, m2, *, lam_init):
    t = q_ref.shape[1]
    qi = pl.program_id(2)
    qt = q_ref[...]
    feat = lax.broadcasted_iota(jnp.int32, qt.shape, 0)
    zero = jnp.zeros_like(qt)
    q_maps = (jnp.where(feat < HEAD_DIM, qt, zero), jnp.where(feat >= HEAD_DIM, qt, zero))
    state = ((acc1, m1), (acc2, m2))
    slot_a, slot_b = (s1a, s2a), (s1b, s2b)
    for acc, m in state:
        acc[...] = jnp.zeros_like(acc)
        m[...] = jnp.full_like(m, MASK_VALUE)

    def scores(j, slot):
        kb = k_ref[pl.ds(pl.multiple_of(j * t, t), t), :]
        for qm, s_buf in zip(q_maps, slot):
            s_buf[...] = _dot(kb, qm)

    def consume(j, slot, masked):
        vb = v_ref[j]
        if masked:
            key = lax.broadcasted_iota(jnp.int32, (t, t), 0)
            query = lax.broadcasted_iota(jnp.int32, (t, t), 1)
            keep = key <= query
        for s_buf, (acc, m) in zip(slot, state):
            load = (lambda: jnp.where(keep, s_buf[...], MASK_VALUE)) if masked else (lambda: s_buf[...])
            m_old = m[...]
            m_new = jnp.maximum(m_old, jnp.max(load(), axis=0, keepdims=True))
            alpha = jnp.exp2(m_old - m_new)
            p = jnp.exp2(load() - m_new)
            acc[...] = alpha * acc[...] + _dot(vb, p.astype(_BF16))
            m[...] = m_new

    scores(0, slot_a)

    def pair(jj, carry):
        j = 2 * jj
        scores(j + 1, slot_b)
        consume(j, slot_a, masked=False)
        scores(j + 2, slot_a)
        consume(j + 1, slot_b, masked=False)
        return carry

    lax.fori_loop(0, qi // 2, pair, 0)

    @pl.when(qi % 2 == 0)
    def _():
        consume(qi, slot_a, masked=True)

    @pl.when(qi % 2 == 1)
    def _():
        scores(qi, slot_b)
        consume(qi - 1, slot_a, masked=False)
        consume(qi, slot_b, masked=True)

    def normalized(acc):
        return acc[:HEAD_WIDTH, :] / acc[HEAD_WIDTH:HEAD_WIDTH + 1, :]

    lam = lam_ref[...]
    lam_full = (jnp.exp(jnp.sum(lam[0:1] * lam[1:2], keepdims=True))
                - jnp.exp(jnp.sum(lam[2:3] * lam[3:4], keepdims=True)) + lam_init)
    o =(normalized(acc1) - lam_full * normalized(acc2)).T
    o = o * lax.rsqrt(jnp.mean(o * o, axis=-1, keepdims=True) + LN_EPS) * g_ref[...]
    o_ref[...] = (o * (1.0 - lam_init)).astype(o_ref.dtype)


def _diff_attn(q_t, k, v_t, lam, g, lam_init):
    bsz, _, seq, _ = k.shape
    t = q_t.shape[-1]
    q_spec = pl.BlockSpec((None, None, None, HEAD_WIDTH, t), lambda b, h, i: (b, h, i, 0, 0))
    k_spec = pl.BlockSpec((None, None, seq, HEAD_WIDTH), lambda b, h, i: (b, h, 0, 0))
    v_spec = pl.BlockSpec((None, None, seq // t, V_ROWS, t), lambda b, h, i: (b, h, 0, 0, 0))
    o_spec = pl.BlockSpec((None, None, t, HEAD_WIDTH), lambda b, h, i: (b, h, i, 0))
    score = pltpu.VMEM((t, t), _F32)
    stat = pltpu.VMEM((1, t), _F32)
    acc = pltpu.VMEM((V_ROWS, t), _F32)
    return pl.pallas_call(
        functools.partial(_diff_attn_kernel, lam_init=lam_init),
        grid=(bsz, N_HEADS, seq // t),
        in_specs=[_resident((4, HEAD_DIM)), _resident((1, HEAD_WIDTH)), q_spec, k_spec, v_spec],
        out_specs=o_spec,
        out_shape=jax.ShapeDtypeStruct(k.shape, _BF16),
        scratch_shapes=[score, score, score, score, acc, acc, stat, stat],
        compiler_params=_params("parallel", "parallel", "arbitrary"),
        name="diff_attn",
    )(lam, g, q_t, k, v_t)


def _mixer_out_kernel(x_ref, hb_ref, ma_ref, gb_ref, wb_ref, wo_ref, g_ref, b_ref, o_ref):
    hb = jnp.concatenate([hb_ref[h] for h in range(N_HEADS)], axis=-1)
    m = ma_ref[...] + gb_ref[...] * _dot(hb, wb_ref[...])
    y = _dot(m.astype(_BF16), wo_ref[...])
    o_ref[...] = _layernorm(ALPHA * x_ref[...] + y, g_ref[...], b_ref[...])


def _mixer_out(x, hb, ma, gb, w_b, w_out, g, b, seq):
    t = x.shape[0]
    tm = min(MIXER_TILE, seq)
    per_seq = seq // tm
    row = pl.BlockSpec((tm, D_MODEL), lambda i: (i, 0))
    head = pl.BlockSpec((None, N_HEADS, tm, HEAD_WIDTH), lambda i: (i // per_seq, 0, i % per_seq, 0))
    return pl.pallas_call(
        _mixer_out_kernel,
        grid=(t // tm,),
        in_specs=[row, head, row, row, _resident((D_MODEL, D_MODEL)), _resident((D_MODEL, D_MODEL)),
                  _resident((1, D_MODEL)), _resident((1, D_MODEL))],
        out_specs=row,
        out_shape=jax.ShapeDtypeStruct((t, D_MODEL), _F32),
        compiler_params=_params("parallel"),
        name="mixer_out",
    )(x, hb, ma, gb, w_b, w_out, g, b)


def _rope_tables(seq):
    half = HEAD_DIM // 2
    inv_freq = ROPE_THETA ** (-jnp.arange(0, HEAD_DIM, 2, dtype=_F32) / HEAD_DIM)
    ang = jnp.arange(seq, dtype=_F32)[:, None] * inv_freq[None, :]
    cos, sin = jnp.cos(ang), jnp.sin(ang)
    zeros = jnp.zeros_like(sin)
    reps = HEAD_WIDTH // HEAD_DIM
    cos_t = jnp.tile(jnp.concatenate([cos, cos], axis=-1), (1, reps))
    sin_a = jnp.tile(jnp.concatenate([-sin, zeros], axis=-1), (1, reps))
    sin_b = jnp.tile(jnp.concatenate([zeros, sin], axis=-1), (1, reps))
    del half
    return cos_t, sin_a, sin_b


def kernel(x, w_in, gate_b, sgu_ln_g, sgu_ln_b, sgu_w, sgu_b, lam, diff_ln_g, w_branch, w_out,
           ffn_w1, ffn_w3, ffn_w2, ln_g, ln_b):
    bsz, seq, _ = x.shape
    cos, sin_a, sin_b = _rope_tables(seq)
    h = x.reshape(bsz * seq, D_MODEL)
    bf = lambda w: w.astype(_BF16)
    for l in range(DEPTH):
        vec = lambda a: a.reshape(1, -1)
        h = _ffn_ln(h, bf(ffn_w1[l, 0]), bf(ffn_w3[l, 0]), bf(ffn_w2[l, 0]),
                    vec(ln_g[l, 0]), vec(ln_b[l, 0]))
        q, k, v, ma, gb = _mixer_in(
            h, bf(w_in[l]), gate_b[l], vec(sgu_ln_g[l]), vec(sgu_ln_b[l]), sgu_w[l],
            sgu_b[l].reshape(N_SGU_GROUPS, CHUNK, 1), bf(w_branch[l, 0]), cos, sin_a, sin_b, bsz, seq)
        hb = _diff_attn(q, k, v, lam[l], vec(diff_ln_g[l]), _lambda_init(l))
        h = _mixer_out(h, hb, ma, gb, bf(w_branch[l, 1]), bf(w_out[l]),
                       vec(ln_g[l, 1]), vec(ln_b[l, 1]), seq)
        h = _ffn_ln(h, bf(ffn_w1[l, 1]), bf(ffn_w3[l, 1]), bf(ffn_w2[l, 1]),
                    vec(ln_g[l, 2]), vec(ln_b[l, 2]))
    return h.reshape(bsz, seq, D_MODEL)
```

```python
import functools
import math

import jax
import jax.numpy as jnp
from jax import lax
from jax.experimental import pallas as pl
from jax.experimental.pallas import tpu as pltpu

D_MODEL = 1024
DEPTH = 4
CHUNK = 128
N_SGU_GROUPS = 8
SGU_GROUP = D_MODEL // N_SGU_GROUPS
HEAD_DIM = 64
N_HEADS = D_MODEL // (2 * HEAD_DIM)
HEAD_WIDTH = 2 * HEAD_DIM
D_FF = 2816
ROPE_THETA = 10000.0
ALPHA = (2.0 * DEPTH) ** 0.25
LN_EPS = 1e-5

OFF_U, OFF_V, OFF_Q, OFF_K, OFF_VAL, OFF_GA, OFF_GB = (
    0, D_MODEL, 2 * D_MODEL, 3 * D_MODEL, 4 * D_MODEL, 5 * D_MODEL, 6 * D_MODEL)
IN_WIDTH = 7 * D_MODEL

VMEM_LIMIT_BYTES = 56 * 1024 * 1024
LANES = 128
BF16_SUBLANES = 16
MASK_VALUE = -1e30
ONES_ROWS = BF16_SUBLANES
V_ROWS = HEAD_WIDTH + ONES_ROWS

FFN_TILE = 512
MIXER_TILE = 512
ATTN_TILE = 512
ATTN_UNROLL = 4
FFN_CHUNKS = ((0, 1280), (1280, D_FF))

_BF16 = jnp.bfloat16
_F32 = jnp.float32


def _lambda_init(layer):
    return 0.8 - 0.6 * math.exp(-0.3 * layer)


def _layernorm(y, g, b):
    mu = jnp.mean(y, axis=-1, keepdims=True)
    d = y - mu
    var = jnp.mean(d * d, axis=-1, keepdims=True)
    return d * lax.rsqrt(var + LN_EPS) * g + b


def _gelu(x):
    return 0.5 * x * (1.0 + lax.erf(x * (2.0 ** -0.5)))


def _dot(a, b):
    return jnp.dot(a, b, preferred_element_type=_F32)


def _resident(shape):
    return pl.BlockSpec(shape, lambda *_: (0,) * len(shape), pipeline_mode=pl.Buffered(1))


def _params(*semantics):
    return pltpu.CompilerParams(dimension_semantics=semantics, vmem_limit_bytes=VMEM_LIMIT_BYTES)


def _ffn_ln_kernel(x_ref, w1_ref, w3_ref, w2_ref, g_ref, b_ref, o_ref):
    x = x_ref[...]
    xb = x.astype(_BF16)
    h = None
    for lo, hi in FFN_CHUNKS:
        gate = _dot(xb, w1_ref[:, lo:hi])
        up = _dot(xb, w3_ref[:, lo:hi])
        act = (gate * jax.nn.sigmoid(gate) * up).astype(_BF16)
        part = _dot(act, w2_ref[lo:hi, :])
        h = part if h is None else h + part
    o_ref[...] = _layernorm(ALPHA * x + 0.5 * h, g_ref[...], b_ref[...])


def _ffn_ln(x, w1, w3, w2, g, b):
    t = x.shape[0]
    tm = min(FFN_TILE, t)
    row = pl.BlockSpec((tm, D_MODEL), lambda i: (i, 0))
    return pl.pallas_call(
        _ffn_ln_kernel,
        grid=(t // tm,),
        in_specs=[row, _resident((D_MODEL, D_FF)), _resident((D_MODEL, D_FF)),
                  _resident((D_FF, D_MODEL)), _resident((1, D_MODEL)), _resident((1, D_MODEL))],
        out_specs=row,
        out_shape=jax.ShapeDtypeStruct((t, D_MODEL), _F32),
        compiler_params=_params("parallel"),
        name="ffn_ln",
    )(x, w1, w3, w2, g, b)


def _mixer_in_kernel(x_ref, w_ref, gate_b_ref, sg_ref, sb_ref, sw_ref, sbias_ref, wa_ref,
                     cos_ref, sin_a_ref, sin_b_ref,
                     q_ref, k_ref, v_ref, ma_ref, gb_ref, ha_ref):
    tm = x_ref.shape[0]
    xb = x_ref[...].astype(_BF16)

    def proj(off, width=D_MODEL):
        return _dot(xb, w_ref[:, off:off + width])

    u = _gelu(proj(OFF_U))
    v = _layernorm(_gelu(proj(OFF_V)), sg_ref[...], sb_ref[...]).astype(_BF16)
    row = lax.broadcasted_iota(jnp.int32, (CHUNK, CHUNK), 0)
    col = lax.broadcasted_iota(jnp.int32, (CHUNK, CHUNK), 1)
    causal = col <= row
    for g in range(N_SGU_GROUPS):
        w_g = jnp.where(causal, sw_ref[g], 0.0).astype(_BF16)
        bias_g = sbias_ref[g]
        cols = slice(g * SGU_GROUP, (g + 1) * SGU_GROUP)
        for c in range(tm // CHUNK):
            rows = slice(c * CHUNK, (c + 1) * CHUNK)
            s = _dot(w_g, v[rows, cols]) + bias_g
            ha_ref[rows, cols] = (u[rows, cols] * s).astype(_BF16)
    gate_a = jax.nn.sigmoid(proj(OFF_GA) + gate_b_ref[0:1, :])
    ma_ref[...] = gate_a * _dot(ha_ref[...], wa_ref[...])
    gb_ref[...] = jax.nn.sigmoid(proj(OFF_GB) + gate_b_ref[1:2, :])

    cos, sin_a, sin_b = cos_ref[...], sin_a_ref[...], sin_b_ref[...]

    def rope(z):
        return (z * cos + pltpu.roll(z, LANES - HEAD_DIM // 2, axis=1) * sin_a
                + pltpu.roll(z, HEAD_DIM // 2, axis=1) * sin_b)

    zq = proj(OFF_Q)
    zk = proj(OFF_K)
    zv = proj(OFF_VAL)
    scale = HEAD_DIM ** -0.5 * math.log2(math.e)
    ones = jnp.ones((ONES_ROWS, tm), _BF16)
    for h in range(N_HEADS):
        cols = slice(h * HEAD_WIDTH, (h + 1) * HEAD_WIDTH)
        q_ref[h] = (rope(zq[:, cols]) * scale).T.astype(_BF16)
        k_ref[h] = rope(zk[:, cols]).astype(_BF16)
        v_ref[h, :HEAD_WIDTH, :] = zv[:, cols].T.astype(_BF16)
        v_ref[h, HEAD_WIDTH:, :] = ones


def _mixer_in(x, w_in, gate_b, sgu_g, sgu_b, sgu_w, sgu_bias, w_a, cos, sin_a, sin_b, bsz, seq):
    t = x.shape[0]
    tm = min(MIXER_TILE, seq)
    per_seq = seq // tm
    row = pl.BlockSpec((tm, D_MODEL), lambda i: (i, 0))
    pos = pl.BlockSpec((tm, HEAD_WIDTH), lambda i: (i % per_seq, 0))
    head = pl.BlockSpec((None, N_HEADS, tm, HEAD_WIDTH), lambda i: (i // per_seq, 0, i % per_seq, 0))
    head_shape = jax.ShapeDtypeStruct((bsz, N_HEADS, seq, HEAD_WIDTH), _BF16)
    def head_t(rows):
        return (pl.BlockSpec((None, N_HEADS, None, rows, tm), lambda i: (i // per_seq, 0, i % per_seq, 0, 0)),
                jax.ShapeDtypeStruct((bsz, N_HEADS, per_seq, rows, tm), _BF16))

    (q_spec, q_shape), (v_spec, v_shape) = head_t(HEAD_WIDTH), head_t(V_ROWS)
    return pl.pallas_call(
        _mixer_in_kernel,
        grid=(t // tm,),
        in_specs=[row, _resident((D_MODEL, IN_WIDTH)), _resident((2, D_MODEL)),
                  _resident((1, D_MODEL)), _resident((1, D_MODEL)),
                  _resident((N_SGU_GROUPS, CHUNK, CHUNK)), _resident((N_SGU_GROUPS, CHUNK, 1)),
                  _resident((D_MODEL, D_MODEL)), pos, pos, pos],
        out_specs=[q_spec, head, v_spec, row, row],
        out_shape=[q_shape, head_shape, v_shape,
                   jax.ShapeDtypeStruct((t, D_MODEL), _F32), jax.ShapeDtypeStruct((t, D_MODEL), _F32)],
        scratch_shapes=[pltpu.VMEM((tm, D_MODEL), _BF16)],
        compiler_params=_params("parallel"),
        name="mixer_in",
    )(x, w_in, gate_b, sgu_g, sgu_b, sgu_w, sgu_bias, w_a, cos, sin_a, sin_b)


def _diff_attn_kernel(lam_ref, g_ref, q_ref, k_ref, v_ref, o_ref,
                      s1a, s1b, s2a, s2b, x1a, x1b, x2a, x2b, acc1, acc2, m1, m2, *, lam_init):
    t = q_ref.shape[1]
    qi = pl.program_id(2)
    qt = q_ref[...]
    feat = lax.broadcasted_iota(jnp.int32, qt.shape, 0)
    zero = jnp.zeros_like(qt)
    q_maps = (jnp.where(feat < HEAD_DIM, qt, zero), jnp.where(feat >= HEAD_DIM, qt, zero))
    state = ((acc1, m1), (acc2, m2))
    slot_a, slot_b = ((s1a, x1a), (s2a, x2a)), ((s1b, x1b), (s2b, x2b))
    for acc, m in state:
        acc[...] = jnp.zeros_like(acc)
        m[...] = jnp.full_like(m, MASK_VALUE)

    def scores(j, slot, masked=False):
        kb = k_ref[pl.ds(pl.multiple_of(j * t, t), t), :]
        for qm, (s_buf, max_buf) in zip(q_maps, slot):
            s = _dot(kb, qm)
            if masked:
                key = lax.broadcasted_iota(jnp.int32, s.shape, 0)
                query = lax.broadcasted_iota(jnp.int32, s.shape, 1)
                s = jnp.where(key <= query, s, MASK_VALUE)
            s_buf[...] = s
            max_buf[...] = jnp.max(s, axis=0, keepdims=True)

    def consume(j, slot):
        vb = v_ref[j]
        for (s_buf, max_buf), (acc, m) in zip(slot, state):
            m_old = m[...]
            m_new = jnp.maximum(m_old, max_buf[...])
            alpha = jnp.exp2(m_old - m_new)
            p = jnp.exp2(s_buf[...] - m_new)
            acc[...] = alpha * acc[...] + _dot(vb, p.astype(_BF16))
            m[...] = m_new

    @pl.when(qi == 0)
    def _():
        scores(0, slot_a, masked=True)

    @pl.when(qi > 0)
    def _():
        scores(0, slot_a)

    def pair(j):
        scores(j + 1, slot_b)
        consume(j, slot_a)
        scores(j + 2, slot_a)
        consume(j + 1, slot_b)

    def quad(jj, carry):
        pair(ATTN_UNROLL * jj)
        pair(ATTN_UNROLL * jj + 2)
        return carry

    n_quads = jnp.maximum(qi - 1, 0) // ATTN_UNROLL
    lax.fori_loop(0, n_quads, quad, 0)
    done = ATTN_UNROLL * n_quads
    extra_pair = qi - done >= 3

    @pl.when(extra_pair)
    def _():
        pair(done)

    left = qi - done - jnp.where(extra_pair, 2, 0)

    @pl.when(left == 0)
    def _():
        consume(qi, slot_a)

    @pl.when(left == 1)
    def _():
        scores(qi, slot_b, masked=True)
        consume(qi - 1, slot_a)
        consume(qi, slot_b)

    @pl.when(left == 2)
    def _():
        scores(qi - 1, slot_b)
        consume(qi - 2, slot_a)
        scores(qi, slot_a, masked=True)
        consume(qi - 1, slot_b)
        consume(qi, slot_a)

    def normalized(acc):
        return acc[:HEAD_WIDTH, :] / acc[HEAD_WIDTH:HEAD_WIDTH + 1, :]

    lam = lam_ref[...]
    lam_full = (jnp.exp(jnp.sum(lam[0:1] * lam[1:2], keepdims=True))
                - jnp.exp(jnp.sum(lam[2:3] * lam[3:4], keepdims=True)) + lam_init)
    o =(normalized(acc1) - lam_full * normalized(acc2)).T
    o = o * lax.rsqrt(jnp.mean(o * o, axis=-1, keepdims=True) + LN_EPS) * g_ref[...]
    o_ref[...] = (o * (1.0 - lam_init)).astype(o_ref.dtype)


def _diff_attn(q_t, k, v_t, lam, g, lam_init):
    bsz, _, seq, _ = k.shape
    t = q_t.shape[-1]
    q_spec = pl.BlockSpec((None, None, None, HEAD_WIDTH, t), lambda b, h, i: (b, h, i, 0, 0))
    k_spec = pl.BlockSpec((None, None, seq, HEAD_WIDTH), lambda b, h, i: (b, h, 0, 0))
    v_spec = pl.BlockSpec((None, None, seq // t, V_ROWS, t), lambda b, h, i: (b, h, 0, 0, 0))
    o_spec = pl.BlockSpec((None, None, t, HEAD_WIDTH), lambda b, h, i: (b, h, i, 0))
    score = pltpu.VMEM((t, t), _F32)
    stat = pltpu.VMEM((1, t), _F32)
    acc = pltpu.VMEM((V_ROWS, t), _F32)
    return pl.pallas_call(
        functools.partial(_diff_attn_kernel, lam_init=lam_init),
        grid=(bsz, N_HEADS, seq // t),
        in_specs=[_resident((4, HEAD_DIM)), _resident((1, HEAD_WIDTH)), q_spec, k_spec, v_spec],
        out_specs=o_spec,
        out_shape=jax.ShapeDtypeStruct(k.shape, _BF16),
        scratch_shapes=[score] * 4 + [stat] * 4 + [acc, acc, stat, stat],
        compiler_params=_params("parallel", "parallel", "arbitrary"),
        name="diff_attn",
    )(lam, g, q_t, k, v_t)


def _mixer_out_kernel(x_ref, hb_ref, ma_ref, gb_ref, wb_ref, wo_ref, g_ref, b_ref, o_ref):
    hb = jnp.concatenate([hb_ref[h] for h in range(N_HEADS)], axis=-1)
    m = ma_ref[...] + gb_ref[...] * _dot(hb, wb_ref[...])
    y = _dot(m.astype(_BF16), wo_ref[...])
    o_ref[...] = _layernorm(ALPHA * x_ref[...] + y, g_ref[...], b_ref[...])


def _mixer_out(x, hb, ma, gb, w_b, w_out, g, b, seq):
    t = x.shape[0]
    tm = min(MIXER_TILE, seq)
    per_seq = seq // tm
    row = pl.BlockSpec((tm, D_MODEL), lambda i: (i, 0))
    head = pl.BlockSpec((None, N_HEADS, tm, HEAD_WIDTH), lambda i: (i // per_seq, 0, i % per_seq, 0))
    return pl.pallas_call(
        _mixer_out_kernel,
        grid=(t // tm,),
        in_specs=[row, head, row, row, _resident((D_MODEL, D_MODEL)), _resident((D_MODEL, D_MODEL)),
                  _resident((1, D_MODEL)), _resident((1, D_MODEL))],
        out_specs=row,
        out_shape=jax.ShapeDtypeStruct((t, D_MODEL), _F32),
        compiler_params=_params("parallel"),
        name="mixer_out",
    )(x, hb, ma, gb, w_b, w_out, g, b)


def _rope_tables(seq):
    half = HEAD_DIM // 2
    inv_freq = ROPE_THETA ** (-jnp.arange(0, HEAD_DIM, 2, dtype=_F32) / HEAD_DIM)
    ang = jnp.arange(seq, dtype=_F32)[:, None] * inv_freq[None, :]
    cos, sin = jnp.cos(ang), jnp.sin(ang)
    zeros = jnp.zeros_like(sin)
    reps = HEAD_WIDTH // HEAD_DIM
    cos_t = jnp.tile(jnp.concatenate([cos, cos], axis=-1), (1, reps))
    sin_a = jnp.tile(jnp.concatenate([-sin, zeros], axis=-1), (1, reps))
    sin_b = jnp.tile(jnp.concatenate([zeros, sin], axis=-1), (1, reps))
    del half
    return cos_t, sin_a, sin_b


def kernel(x, w_in, gate_b, sgu_ln_g, sgu_ln_b, sgu_w, sgu_b, lam, diff_ln_g, w_branch, w_out,
           ffn_w1, ffn_w3, ffn_w2, ln_g, ln_b):
    bsz, seq, _ = x.shape
    cos, sin_a, sin_b = _rope_tables(seq)
    h = x.reshape(bsz * seq, D_MODEL)
    bf = lambda w: w.astype(_BF16)
    for l in range(DEPTH):
        vec = lambda a: a.reshape(1, -1)
        h = _ffn_ln(h, bf(ffn_w1[l, 0]), bf(ffn_w3[l, 0]), bf(ffn_w2[l, 0]),
                    vec(ln_g[l, 0]), vec(ln_b[l, 0]))
        q, k, v, ma, gb = _mixer_in(
            h, bf(w_in[l]), gate_b[l], vec(sgu_ln_g[l]), vec(sgu_ln_b[l]), sgu_w[l],
            sgu_b[l].reshape(N_SGU_GROUPS, CHUNK, 1), bf(w_branch[l, 0]), cos, sin_a, sin_b, bsz, seq)
        hb = _diff_attn(q, k, v, lam[l], vec(diff_ln_g[l]), _lambda_init(l))
        h = _mixer_out(h, hb, ma, gb, bf(w_branch[l, 1]), bf(w_out[l]),
                       vec(ln_g[l, 1]), vec(ln_b[l, 1]), seq)
        h = _ffn_ln(h, bf(ffn_w1[l, 1]), bf(ffn_w3[l, 1]), bf(ffn_w2[l, 1]),
                    vec(ln_g[l, 2]), vec(ln_b[l, 2]))
    return h.reshape(bsz, seq, D_MODEL)
```

```python
import functools
import math

import jax
import jax.numpy as jnp
from jax import lax
from jax.experimental import pallas as pl
from jax.experimental.pallas import tpu as pltpu

D_MODEL = 1024
DEPTH = 4
CHUNK = 128
N_SGU_GROUPS = 8
SGU_GROUP = D_MODEL // N_SGU_GROUPS
HEAD_DIM = 64
N_HEADS = D_MODEL // (2 * HEAD_DIM)
HEAD_WIDTH = 2 * HEAD_DIM
D_FF = 2816
ROPE_THETA = 10000.0
ALPHA = (2.0 * DEPTH) ** 0.25
LN_EPS = 1e-5

OFF_U, OFF_V, OFF_Q, OFF_K, OFF_VAL, OFF_GA, OFF_GB = (
    0, D_MODEL, 2 * D_MODEL, 3 * D_MODEL, 4 * D_MODEL, 5 * D_MODEL, 6 * D_MODEL)
IN_WIDTH = 7 * D_MODEL

VMEM_LIMIT_BYTES = 56 * 1024 * 1024
LANES = 128
BF16_SUBLANES = 16
MASK_VALUE = -1e30
ONES_ROWS = BF16_SUBLANES
V_ROWS = HEAD_WIDTH + ONES_ROWS

FFN_TILE = 512
MIXER_TILE = 512
ATTN_Q_BLOCKS = 1
ATTN_UNROLL = 8
FFN_CHUNKS = ((0, 1280), (1280, D_FF))

_BF16 = jnp.bfloat16
_F32 = jnp.float32


def _lambda_init(layer):
    return 0.8 - 0.6 * math.exp(-0.3 * layer)


def _layernorm(y, g, b):
    mu = jnp.mean(y, axis=-1, keepdims=True)
    d = y - mu
    var = jnp.mean(d * d, axis=-1, keepdims=True)
    return d * lax.rsqrt(var + LN_EPS) * g + b


def _gelu(x):
    return 0.5 * x * (1.0 + lax.erf(x * (2.0 ** -0.5)))


def _dot(a, b):
    return jnp.dot(a, b, preferred_element_type=_F32)


def _resident(shape):
    return pl.BlockSpec(shape, lambda *_: (0,) * len(shape), pipeline_mode=pl.Buffered(1))


def _params(*semantics):
    return pltpu.CompilerParams(dimension_semantics=semantics, vmem_limit_bytes=VMEM_LIMIT_BYTES)


def _ffn_ln_kernel(x_ref, w1_ref, w3_ref, w2_ref, g_ref, b_ref, o_ref):
    x = x_ref[...]
    xb = x.astype(_BF16)
    h = None
    for lo, hi in FFN_CHUNKS:
        gate = _dot(xb, w1_ref[:, lo:hi])
        up = _dot(xb, w3_ref[:, lo:hi])
        act = (gate * jax.nn.sigmoid(gate) * up).astype(_BF16)
        part = _dot(act, w2_ref[lo:hi, :])
        h = part if h is None else h + part
    o_ref[...] = _layernorm(ALPHA * x + 0.5 * h, g_ref[...], b_ref[...])


def _ffn_ln(x, w1, w3, w2, g, b):
    t = x.shape[0]
    tm = min(FFN_TILE, t)
    row = pl.BlockSpec((tm, D_MODEL), lambda i: (i, 0))
    return pl.pallas_call(
        _ffn_ln_kernel,
        grid=(t // tm,),
        in_specs=[row, _resident((D_MODEL, D_FF)), _resident((D_MODEL, D_FF)),
                  _resident((D_FF, D_MODEL)), _resident((1, D_MODEL)), _resident((1, D_MODEL))],
        out_specs=row,
        out_shape=jax.ShapeDtypeStruct((t, D_MODEL), _F32),
        compiler_params=_params("parallel"),
        name="ffn_ln",
    )(x, w1, w3, w2, g, b)


def _mixer_in_kernel(x_ref, w_ref, gate_b_ref, sg_ref, sb_ref, sw_ref, sbias_ref, wa_ref,
                     cos_ref, sin_a_ref, sin_b_ref,
                     q_ref, k_ref, v_ref, ma_ref, gb_ref, ha_ref):
    tm = x_ref.shape[0]
    xb = x_ref[...].astype(_BF16)

    def proj(off, width=D_MODEL):
        return _dot(xb, w_ref[:, off:off + width])

    u = _gelu(proj(OFF_U))
    v = _layernorm(_gelu(proj(OFF_V)), sg_ref[...], sb_ref[...]).astype(_BF16)
    row = lax.broadcasted_iota(jnp.int32, (CHUNK, CHUNK), 0)
    col = lax.broadcasted_iota(jnp.int32, (CHUNK, CHUNK), 1)
    causal = col <= row
    for g in range(N_SGU_GROUPS):
        w_g = jnp.where(causal, sw_ref[g], 0.0).astype(_BF16)
        bias_g = sbias_ref[g]
        cols = slice(g * SGU_GROUP, (g + 1) * SGU_GROUP)
        for c in range(tm // CHUNK):
            rows = slice(c * CHUNK, (c + 1) * CHUNK)
            s = _dot(w_g, v[rows, cols]) + bias_g
            ha_ref[rows, cols] = (u[rows, cols] * s).astype(_BF16)
    gate_a = jax.nn.sigmoid(proj(OFF_GA) + gate_b_ref[0:1, :])
    ma_ref[...] = gate_a * _dot(ha_ref[...], wa_ref[...])
    gb_ref[...] = jax.nn.sigmoid(proj(OFF_GB) + gate_b_ref[1:2, :])

    cos, sin_a, sin_b = cos_ref[...], sin_a_ref[...], sin_b_ref[...]

    def rope(z):
        return (z * cos + pltpu.roll(z, LANES - HEAD_DIM // 2, axis=1) * sin_a
                + pltpu.roll(z, HEAD_DIM // 2, axis=1) * sin_b)

    zq = proj(OFF_Q)
    zk = proj(OFF_K)
    zv = proj(OFF_VAL)
    scale = HEAD_DIM ** -0.5 * math.log2(math.e)
    ones = jnp.ones((ONES_ROWS, tm), _BF16)
    for h in range(N_HEADS):
        cols = slice(h * HEAD_WIDTH, (h + 1) * HEAD_WIDTH)
        q_ref[h] = (rope(zq[:, cols]) * scale).T.astype(_BF16)
        k_ref[h] = rope(zk[:, cols]).astype(_BF16)
        v_ref[h, :HEAD_WIDTH, :] = zv[:, cols].T.astype(_BF16)
        v_ref[h, HEAD_WIDTH:, :] = ones


def _mixer_in(x, w_in, gate_b, sgu_g, sgu_b, sgu_w, sgu_bias, w_a, cos, sin_a, sin_b, bsz, seq):
    t = x.shape[0]
    tm = min(MIXER_TILE, seq)
    per_seq = seq // tm
    row = pl.BlockSpec((tm, D_MODEL), lambda i: (i, 0))
    pos = pl.BlockSpec((tm, HEAD_WIDTH), lambda i: (i % per_seq, 0))
    head = pl.BlockSpec((None, N_HEADS, tm, HEAD_WIDTH), lambda i: (i // per_seq, 0, i % per_seq, 0))
    head_shape = jax.ShapeDtypeStruct((bsz, N_HEADS, seq, HEAD_WIDTH), _BF16)
    def head_t(rows):
        return (pl.BlockSpec((None, N_HEADS, None, rows, tm), lambda i: (i // per_seq, 0, i % per_seq, 0, 0)),
                jax.ShapeDtypeStruct((bsz, N_HEADS, per_seq, rows, tm), _BF16))

    (q_spec, q_shape), (v_spec, v_shape) = head_t(HEAD_WIDTH), head_t(V_ROWS)
    return pl.pallas_call(
        _mixer_in_kernel,
        grid=(t // tm,),
        in_specs=[row, _resident((D_MODEL, IN_WIDTH)), _resident((2, D_MODEL)),
                  _resident((1, D_MODEL)), _resident((1, D_MODEL)),
                  _resident((N_SGU_GROUPS, CHUNK, CHUNK)), _resident((N_SGU_GROUPS, CHUNK, 1)),
                  _resident((D_MODEL, D_MODEL)), pos, pos, pos],
        out_specs=[q_spec, head, v_spec, row, row],
        out_shape=[q_shape, head_shape, v_shape,
                   jax.ShapeDtypeStruct((t, D_MODEL), _F32), jax.ShapeDtypeStruct((t, D_MODEL), _F32)],
        scratch_shapes=[pltpu.VMEM((tm, D_MODEL), _BF16)],
        compiler_params=_params("parallel"),
        name="mixer_in",
    )(x, w_in, gate_b, sgu_g, sgu_b, sgu_w, sgu_bias, w_a, cos, sin_a, sin_b)


def _diff_attn_kernel(lam_ref, g_ref, q_ref, k_ref, v_ref, o_ref,
                      s1a, s1b, s2a, s2b, x1a, x1b, x2a, x2b, acc1, acc2, m1, m2, *, lam_init):
    tk = k_ref.shape[0] // v_ref.shape[0]
    step = pl.program_id(2)
    n_tiles = pl.num_programs(2) - 1
    active = step < n_tiles
    qi = jnp.minimum(step, n_tiles - 1)
    qt = jnp.concatenate([q_ref[r] for r in range(ATTN_Q_BLOCKS)], axis=1)
    feat = lax.broadcasted_iota(jnp.int32, qt.shape, 0)
    zero = jnp.zeros_like(qt)
    q_maps = (jnp.where(feat < HEAD_DIM, qt, zero), jnp.where(feat >= HEAD_DIM, qt, zero))
    state = ((acc1, m1), (acc2, m2))
    slot_a, slot_b = ((s1a, x1a), (s2a, x2a)), ((s1b, x1b), (s2b, x2b))

    def init_state():
        for acc, m in state:
            acc[...] = jnp.zeros_like(acc)
            m[...] = jnp.full_like(m, MASK_VALUE)

    def store_previous_tile():
        def normalized(acc):
            return acc[:HEAD_WIDTH, :] / acc[HEAD_WIDTH:HEAD_WIDTH + 1, :]

        lam = lam_ref[...]
        lam_full = (jnp.exp(jnp.sum(lam[0:1] * lam[1:2], keepdims=True))
                    - jnp.exp(jnp.sum(lam[2:3] * lam[3:4], keepdims=True)) + lam_init)
        o = (normalized(acc1) - lam_full * normalized(acc2)).T
        o = o * lax.rsqrt(jnp.mean(o * o, axis=-1, keepdims=True) + LN_EPS) * g_ref[...]
        o_ref[...] = (o * (1.0 - lam_init)).astype(o_ref.dtype)

    def scores(j, slot, diag=None):
        kb = k_ref[pl.ds(pl.multiple_of(j * tk, tk), tk), :]
        for qm, (s_buf, max_buf) in zip(q_maps, slot):
            s = _dot(kb, qm)
            if diag is not None:
                key = lax.broadcasted_iota(jnp.int32, s.shape, 0) + diag * tk
                query = lax.broadcasted_iota(jnp.int32, s.shape, 1)
                s = jnp.where(key <= query, s, MASK_VALUE)
            s_buf[...] = s
            max_buf[...] = jnp.max(s, axis=0, keepdims=True)

    def consume(j, slot):
        vb = v_ref[j]
        for (s_buf, max_buf), (acc, m) in zip(slot, state):
            m_old = m[...]
            m_new = jnp.maximum(m_old, max_buf[...])
            alpha = jnp.exp2(m_old - m_new)
            p = jnp.exp2(s_buf[...] - m_new)
            acc[...] = alpha * acc[...] + _dot(vb, p.astype(_BF16))
            m[...] = m_new

    first_diag = ATTN_Q_BLOCKS * qi
    slots = (slot_a, slot_b)

    @pl.when(step == 0)
    def _():
        scores(0, slot_a, diag=0)
        init_state()

    @pl.when((step > 0) & active)
    def _():
        scores(0, slot_a)
        store_previous_tile()
        init_state()

    @pl.when(step == n_tiles)
    def _():
        store_previous_tile()

    def run(first, count, unmasked, last_scores=True):
        for i in range(count):
            nxt = i + 1
            if nxt < count or last_scores:
                scores(first + nxt, slots[nxt % 2], diag=None if nxt < unmasked else nxt - unmasked)
            consume(first + i, slots[i % 2])

    n_trips = jnp.where(active, jnp.maximum(first_diag - 1, 0) // ATTN_UNROLL, 0)

    def trip(i, carry):
        run(ATTN_UNROLL * i, ATTN_UNROLL, ATTN_UNROLL + 1)
        return carry

    lax.fori_loop(0, n_trips, trip, 0)
    done = ATTN_UNROLL * n_trips
    size = ATTN_UNROLL // 2
    while size >= 2:
        fits = active & (first_diag - done > size)

        @pl.when(fits)
        def _(done=done, size=size):
            run(done, size, size + 1)

        done = done + jnp.where(fits, size, 0)
        size //= 2

    left = first_diag - done
    for n_left in range(3):
        @pl.when(active & (left == n_left))
        def _(n_left=n_left):
            run(first_diag - n_left, n_left + ATTN_Q_BLOCKS, n_left, last_scores=False)


def _diff_attn(q_t, k, v_t, lam, g, lam_init):
    bsz, _, seq, _ = k.shape
    tk = q_t.shape[-1]
    tq = ATTN_Q_BLOCKS * tk
    n_tiles = seq // tq
    q_spec = pl.BlockSpec((None, None, ATTN_Q_BLOCKS, HEAD_WIDTH, tk),
                          lambda b, h, i: (b, h, jnp.minimum(i, n_tiles - 1), 0, 0))
    k_spec = pl.BlockSpec((None, None, seq, HEAD_WIDTH), lambda b, h, i: (b, h, 0, 0))
    v_spec = pl.BlockSpec((None, None, seq // tk, V_ROWS, tk), lambda b, h, i: (b, h, 0, 0, 0))
    o_spec = pl.BlockSpec((None, None, tq, HEAD_WIDTH), lambda b, h, i: (b, h, jnp.maximum(i - 1, 0), 0))
    score = pltpu.VMEM((tk, tq), _F32)
    stat = pltpu.VMEM((1, tq), _F32)
    acc = pltpu.VMEM((V_ROWS, tq), _F32)
    return pl.pallas_call(
        functools.partial(_diff_attn_kernel, lam_init=lam_init),
        grid=(bsz, N_HEADS, n_tiles + 1),
        in_specs=[_resident((4, HEAD_DIM)), _resident((1, HEAD_WIDTH)), q_spec, k_spec, v_spec],
        out_specs=o_spec,
        out_shape=jax.ShapeDtypeStruct(k.shape, _BF16),
        scratch_shapes=[score] * 4 + [stat] * 4 + [acc, acc, stat, stat],
        compiler_params=_params("parallel", "parallel", "arbitrary"),
        name="diff_attn",
    )(lam, g, q_t, k, v_t)


def _mixer_out_kernel(x_ref, hb_ref, ma_ref, gb_ref, wb_ref, wo_ref, g_ref, b_ref, o_ref):
    hb = jnp.concatenate([hb_ref[h] for h in range(N_HEADS)], axis=-1)
    m = ma_ref[...] + gb_ref[...] * _dot(hb, wb_ref[...])
    y = _dot(m.astype(_BF16), wo_ref[...])
    o_ref[...] = _layernorm(ALPHA * x_ref[...] + y, g_ref[...], b_ref[...])


def _mixer_out(x, hb, ma, gb, w_b, w_out, g, b, seq):
    t = x.shape[0]
    tm = min(MIXER_TILE, seq)
    per_seq = seq // tm
    row = pl.BlockSpec((tm, D_MODEL), lambda i: (i, 0))
    head = pl.BlockSpec((None, N_HEADS, tm, HEAD_WIDTH), lambda i: (i // per_seq, 0, i % per_seq, 0))
    return pl.pallas_call(
        _mixer_out_kernel,
        grid=(t // tm,),
        in_specs=[row, head, row, row, _resident((D_MODEL, D_MODEL)), _resident((D_MODEL, D_MODEL)),
                  _resident((1, D_MODEL)), _resident((1, D_MODEL))],
        out_specs=row,
        out_shape=jax.ShapeDtypeStruct((t, D_MODEL), _F32),
        compiler_params=_params("parallel"),
        name="mixer_out",
    )(x, hb, ma, gb, w_b, w_out, g, b)


def _rope_tables(seq):
    half = HEAD_DIM // 2
    inv_freq = ROPE_THETA ** (-jnp.arange(0, HEAD_DIM, 2, dtype=_F32) / HEAD_DIM)
    ang = jnp.arange(seq, dtype=_F32)[:, None] * inv_freq[None, :]
    cos, sin = jnp.cos(ang), jnp.sin(ang)
    zeros = jnp.zeros_like(sin)
    reps = HEAD_WIDTH // HEAD_DIM
    cos_t = jnp.tile(jnp.concatenate([cos, cos], axis=-1), (1, reps))
    sin_a = jnp.tile(jnp.concatenate([-sin, zeros], axis=-1), (1, reps))
    sin_b = jnp.tile(jnp.concatenate([zeros, sin], axis=-1), (1, reps))
    del half
    return cos_t, sin_a, sin_b


def kernel(x, w_in, gate_b, sgu_ln_g, sgu_ln_b, sgu_w, sgu_b, lam, diff_ln_g, w_branch, w_out,
           ffn_w1, ffn_w3, ffn_w2, ln_g, ln_b):
    bsz, seq, _ = x.shape
    cos, sin_a, sin_b = _rope_tables(seq)
    h = x.reshape(bsz * seq, D_MODEL)
    bf = lambda w: w.astype(_BF16)
    for l in range(DEPTH):
        vec = lambda a: a.reshape(1, -1)
        h = _ffn_ln(h, bf(ffn_w1[l, 0]), bf(ffn_w3[l, 0]), bf(ffn_w2[l, 0]),
                    vec(ln_g[l, 0]), vec(ln_b[l, 0]))
        q, k, v, ma, gb = _mixer_in(
            h, bf(w_in[l]), gate_b[l], vec(sgu_ln_g[l]), vec(sgu_ln_b[l]), sgu_w[l],
            sgu_b[l].reshape(N_SGU_GROUPS, CHUNK, 1), bf(w_branch[l, 0]), cos, sin_a, sin_b, bsz, seq)
        hb = _diff_attn(q, k, v, lam[l], vec(diff_ln_g[l]), _lambda_init(l))
        h = _mixer_out(h, hb, ma, gb, bf(w_branch[l, 1]), bf(w_out[l]),
                       vec(ln_g[l, 1]), vec(ln_b[l, 1]), seq)
        h = _ffn_ln(h, bf(ffn_w1[l, 1]), bf(ffn_w3[l, 1]), bf(ffn_w2[l, 1]),
                    vec(ln_g[l, 2]), vec(ln_b[l, 2]))
    return h.reshape(bsz, seq, D_MODEL)
```

```python
import functools
import math

import jax
import jax.numpy as jnp
from jax import lax
from jax.experimental import pallas as pl
from jax.experimental.pallas import tpu as pltpu

D_MODEL = 1024
DEPTH = 4
CHUNK = 128
N_SGU_GROUPS = 8
SGU_GROUP = D_MODEL // N_SGU_GROUPS
HEAD_DIM = 64
N_HEADS = D_MODEL // (2 * HEAD_DIM)
HEAD_WIDTH = 2 * HEAD_DIM
D_FF = 2816
ROPE_THETA = 10000.0
ALPHA = (2.0 * DEPTH) ** 0.25
LN_EPS = 1e-5

OFF_U, OFF_V, OFF_Q, OFF_K, OFF_VAL, OFF_GA, OFF_GB = (
    0, D_MODEL, 2 * D_MODEL, 3 * D_MODEL, 4 * D_MODEL, 5 * D_MODEL, 6 * D_MODEL)
IN_WIDTH = 7 * D_MODEL

VMEM_LIMIT_BYTES = 56 * 1024 * 1024
LANES = 128
BF16_SUBLANES = 16
MASK_VALUE = -1e30
ONES_ROWS = BF16_SUBLANES
V_ROWS = HEAD_WIDTH + ONES_ROWS

FFN_TILE = 1024
MIXER_TILE = 512
MIXER_OUT_TILE = 1024
ATTN_Q_BLOCKS = 1
ATTN_UNROLL = 8
FFN_CHUNKS = ((0, 1024), (1024, 2048), (2048, D_FF))

_BF16 = jnp.bfloat16
_F32 = jnp.float32


def _lambda_init(layer):
    return 0.8 - 0.6 * math.exp(-0.3 * layer)


def _layernorm(y, g, b):
    mu = jnp.mean(y, axis=-1, keepdims=True)
    d = y - mu
    var = jnp.mean(d * d, axis=-1, keepdims=True)
    return d * lax.rsqrt(var + LN_EPS) * g + b


def _gelu(x):
    return 0.5 * x * (1.0 + lax.erf(x * (2.0 ** -0.5)))


def _dot(a, b):
    return jnp.dot(a, b, preferred_element_type=_F32)


def _resident(shape):
    return pl.BlockSpec(shape, lambda *_: (0,) * len(shape), pipeline_mode=pl.Buffered(1))


def _params(*semantics):
    return pltpu.CompilerParams(dimension_semantics=semantics, vmem_limit_bytes=VMEM_LIMIT_BYTES)


def _ffn_ln_kernel(x_ref, w1_ref, w3_ref, w2_ref, g_ref, b_ref, o_ref):
    x = x_ref[...]
    xb = x.astype(_BF16)
    h = None
    for lo, hi in FFN_CHUNKS:
        gate = _dot(xb, w1_ref[:, lo:hi])
        up = _dot(xb, w3_ref[:, lo:hi])
        act = (gate * jax.nn.sigmoid(gate) * up).astype(_BF16)
        part = _dot(act, w2_ref[lo:hi, :])
        h = part if h is None else h + part
    o_ref[...] = _layernorm(ALPHA * x + 0.5 * h, g_ref[...], b_ref[...])


def _ffn_ln(x, w1, w3, w2, g, b):
    t = x.shape[0]
    tm = min(FFN_TILE, t)
    row = pl.BlockSpec((tm, D_MODEL), lambda i: (i, 0))
    return pl.pallas_call(
        _ffn_ln_kernel,
        grid=(t // tm,),
        in_specs=[row, _resident((D_MODEL, D_FF)), _resident((D_MODEL, D_FF)),
                  _resident((D_FF, D_MODEL)), _resident((1, D_MODEL)), _resident((1, D_MODEL))],
        out_specs=row,
        out_shape=jax.ShapeDtypeStruct((t, D_MODEL), _F32),
        compiler_params=_params("parallel"),
        name="ffn_ln",
    )(x, w1, w3, w2, g, b)


def _mixer_in_kernel(x_ref, w_ref, gate_b_ref, sg_ref, sb_ref, sw_ref, sbias_ref, wa_ref,
                     cos_ref, sin_a_ref, sin_b_ref,
                     q_ref, k_ref, v_ref, ma_ref, gb_ref, ha_ref):
    tm = x_ref.shape[0]
    xb = x_ref[...].astype(_BF16)

    def proj(off, width=D_MODEL):
        return _dot(xb, w_ref[:, off:off + width])

    u = _gelu(proj(OFF_U))
    v = _layernorm(_gelu(proj(OFF_V)), sg_ref[...], sb_ref[...]).astype(_BF16)
    row = lax.broadcasted_iota(jnp.int32, (CHUNK, CHUNK), 0)
    col = lax.broadcasted_iota(jnp.int32, (CHUNK, CHUNK), 1)
    causal = col <= row
    for g in range(N_SGU_GROUPS):
        w_g = jnp.where(causal, sw_ref[g], 0.0).astype(_BF16)
        bias_g = sbias_ref[g]
        cols = slice(g * SGU_GROUP, (g + 1) * SGU_GROUP)
        for c in range(tm // CHUNK):
            rows = slice(c * CHUNK, (c + 1) * CHUNK)
            s = _dot(w_g, v[rows, cols]) + bias_g
            ha_ref[rows, cols] = (u[rows, cols] * s).astype(_BF16)
    gate_a = jax.nn.sigmoid(proj(OFF_GA) + gate_b_ref[0:1, :])
    ma_ref[...] = (gate_a * _dot(ha_ref[...], wa_ref[...])).astype(ma_ref.dtype)
    gb_ref[...] = jax.nn.sigmoid(proj(OFF_GB) + gate_b_ref[1:2, :]).astype(gb_ref.dtype)

    cos, sin_a, sin_b = cos_ref[...], sin_a_ref[...], sin_b_ref[...]

    def rope(z):
        return (z * cos + pltpu.roll(z, LANES - HEAD_DIM // 2, axis=1) * sin_a
                + pltpu.roll(z, HEAD_DIM // 2, axis=1) * sin_b)

    zq = proj(OFF_Q)
    zk = proj(OFF_K)
    zv = proj(OFF_VAL)
    scale = HEAD_DIM ** -0.5 * math.log2(math.e)
    ones = jnp.ones((ONES_ROWS, tm), _BF16)
    for h in range(N_HEADS):
        cols = slice(h * HEAD_WIDTH, (h + 1) * HEAD_WIDTH)
        q_ref[h] = (rope(zq[:, cols]) * scale).T.astype(_BF16)
        k_ref[h] = rope(zk[:, cols]).astype(_BF16)
        v_ref[h, :HEAD_WIDTH, :] = zv[:, cols].T.astype(_BF16)
        v_ref[h, HEAD_WIDTH:, :] = ones


def _mixer_in(x, w_in, gate_b, sgu_g, sgu_b, sgu_w, sgu_bias, w_a, cos, sin_a, sin_b, bsz, seq):
    t = x.shape[0]
    tm = min(MIXER_TILE, seq)
    per_seq = seq // tm
    row = pl.BlockSpec((tm, D_MODEL), lambda i: (i, 0))
    pos = pl.BlockSpec((tm, HEAD_WIDTH), lambda i: (i % per_seq, 0))
    head = pl.BlockSpec((None, N_HEADS, tm, HEAD_WIDTH), lambda i: (i // per_seq, 0, i % per_seq, 0))
    head_shape = jax.ShapeDtypeStruct((bsz, N_HEADS, seq, HEAD_WIDTH), _BF16)
    def head_t(rows):
        return (pl.BlockSpec((None, N_HEADS, None, rows, tm), lambda i: (i // per_seq, 0, i % per_seq, 0, 0)),
                jax.ShapeDtypeStruct((bsz, N_HEADS, per_seq, rows, tm), _BF16))

    (q_spec, q_shape), (v_spec, v_shape) = head_t(HEAD_WIDTH), head_t(V_ROWS)
    return pl.pallas_call(
        _mixer_in_kernel,
        grid=(t // tm,),
        in_specs=[row, _resident((D_MODEL, IN_WIDTH)), _resident((2, D_MODEL)),
                  _resident((1, D_MODEL)), _resident((1, D_MODEL)),
                  _resident((N_SGU_GROUPS, CHUNK, CHUNK)), _resident((N_SGU_GROUPS, CHUNK, 1)),
                  _resident((D_MODEL, D_MODEL)), pos, pos, pos],
        out_specs=[q_spec, head, v_spec, row, row],
        out_shape=[q_shape, head_shape, v_shape,
                   jax.ShapeDtypeStruct((t, D_MODEL), _BF16), jax.ShapeDtypeStruct((t, D_MODEL), _BF16)],
        scratch_shapes=[pltpu.VMEM((tm, D_MODEL), _BF16)],
        compiler_params=_params("parallel"),
        name="mixer_in",
    )(x, w_in, gate_b, sgu_g, sgu_b, sgu_w, sgu_bias, w_a, cos, sin_a, sin_b)


def _diff_attn_kernel(lam_ref, g_ref, q_ref, k_ref, v_ref, o_ref,
                      s1a, s1b, s2a, s2b, x1a, x1b, x2a, x2b, acc1, acc2, m1, m2, *, lam_init):
    tk = k_ref.shape[0] // v_ref.shape[0]
    qi = pl.program_id(2)
    n_tiles = pl.num_programs(2)
    tq = o_ref.shape[0] // 2
    qt = jnp.concatenate([q_ref[r] for r in range(ATTN_Q_BLOCKS)], axis=1)
    feat = lax.broadcasted_iota(jnp.int32, qt.shape, 0)
    zero = jnp.zeros_like(qt)
    q_maps = (jnp.where(feat < HEAD_DIM, qt, zero), jnp.where(feat >= HEAD_DIM, qt, zero))
    state = ((acc1, m1), (acc2, m2))
    slot_a, slot_b = ((s1a, x1a), (s2a, x2a)), ((s1b, x1b), (s2b, x2b))

    def init_state():
        for acc, m in state:
            acc[...] = jnp.zeros_like(acc)
            m[...] = jnp.full_like(m, MASK_VALUE)

    def store_tile(tile):
        def normalized(acc):
            return acc[:HEAD_WIDTH, :] / acc[HEAD_WIDTH:HEAD_WIDTH + 1, :]

        lam = lam_ref[...]
        lam_full = (jnp.exp(jnp.sum(lam[0:1] * lam[1:2], keepdims=True))
                    - jnp.exp(jnp.sum(lam[2:3] * lam[3:4], keepdims=True)) + lam_init)
        o = (normalized(acc1) - lam_full * normalized(acc2)).T
        o = o * lax.rsqrt(jnp.mean(o * o, axis=-1, keepdims=True) + LN_EPS) * g_ref[...]
        rows = pl.ds(pl.multiple_of((tile % 2) * tq, tq), tq)
        o_ref[rows, :] = (o * (1.0 - lam_init)).astype(o_ref.dtype)

    def scores(j, slot, diag=None):
        kb = k_ref[pl.ds(pl.multiple_of(j * tk, tk), tk), :]
        for qm, (s_buf, max_buf) in zip(q_maps, slot):
            s = _dot(kb, qm)
            if diag is not None:
                key = lax.broadcasted_iota(jnp.int32, s.shape, 0) + diag * tk
                query = lax.broadcasted_iota(jnp.int32, s.shape, 1)
                s = jnp.where(key <= query, s, MASK_VALUE)
            s_buf[...] = s
            max_buf[...] = jnp.max(s, axis=0, keepdims=True)

    def consume(j, slot):
        vb = v_ref[j]
        for (s_buf, max_buf), (acc, m) in zip(slot, state):
            m_old = m[...]
            m_new = jnp.maximum(m_old, max_buf[...])
            alpha = jnp.exp2(m_old - m_new)
            p = jnp.exp2(s_buf[...] - m_new)
            acc[...] = alpha * acc[...] + _dot(vb, p.astype(_BF16))
            m[...] = m_new

    first_diag = ATTN_Q_BLOCKS * qi
    slots = (slot_a, slot_b)

    @pl.when(qi == 0)
    def _():
        scores(0, slot_a, diag=0)
        init_state()

    @pl.when(qi > 0)
    def _():
        scores(0, slot_a)
        store_tile(qi - 1)
        init_state()

    def run(first, count, unmasked, last_scores=True):
        for i in range(count):
            nxt = i + 1
            if nxt < count or last_scores:
                scores(first + nxt, slots[nxt % 2], diag=None if nxt < unmasked else nxt - unmasked)
            consume(first + i, slots[i % 2])

    n_trips = jnp.maximum(first_diag - 1, 0) // ATTN_UNROLL

    def trip(i, carry):
        run(ATTN_UNROLL * i, ATTN_UNROLL, ATTN_UNROLL + 1)
        return carry

    lax.fori_loop(0, n_trips, trip, 0)
    done = ATTN_UNROLL * n_trips
    size = ATTN_UNROLL // 2
    while size >= 2:
        fits = first_diag - done > size

        @pl.when(fits)
        def _(done=done, size=size):
            run(done, size, size + 1)

        done = done + jnp.where(fits, size, 0)
        size //= 2

    left = first_diag - done
    for n_left in range(3):
        @pl.when(left == n_left)
        def _(n_left=n_left):
            run(first_diag - n_left, n_left + ATTN_Q_BLOCKS, n_left, last_scores=False)

    @pl.when(qi == n_tiles - 1)
    def _():
        store_tile(qi)


def _diff_attn(q_t, k, v_t, lam, g, lam_init):
    bsz, _, seq, _ = k.shape
    tk = q_t.shape[-1]
    tq = ATTN_Q_BLOCKS * tk
    n_tiles = seq // tq
    assert n_tiles % 2 == 0, "output blocks hold pairs of query tiles"
    q_spec = pl.BlockSpec((None, None, ATTN_Q_BLOCKS, HEAD_WIDTH, tk), lambda b, h, i: (b, h, i, 0, 0))
    k_spec = pl.BlockSpec((None, None, seq, HEAD_WIDTH), lambda b, h, i: (b, h, 0, 0))
    v_spec = pl.BlockSpec((None, None, seq // tk, V_ROWS, tk), lambda b, h, i: (b, h, 0, 0, 0))
    o_spec = pl.BlockSpec((None, None, 2 * tq, HEAD_WIDTH),
                          lambda b, h, i: (b, h, jnp.maximum(i - 1, 0) // 2, 0))
    score = pltpu.VMEM((tk, tq), _F32)
    stat = pltpu.VMEM((1, tq), _F32)
    acc = pltpu.VMEM((V_ROWS, tq), _F32)
    return pl.pallas_call(
        functools.partial(_diff_attn_kernel, lam_init=lam_init),
        grid=(bsz, N_HEADS, n_tiles),
        in_specs=[_resident((4, HEAD_DIM)), _resident((1, HEAD_WIDTH)), q_spec, k_spec, v_spec],
        out_specs=o_spec,
        out_shape=jax.ShapeDtypeStruct(k.shape, _BF16),
        scratch_shapes=[score] * 4 + [stat] * 4 + [acc, acc, stat, stat],
        compiler_params=_params("parallel", "parallel", "arbitrary"),
        name="diff_attn",
    )(lam, g, q_t, k, v_t)


def _mixer_out_kernel(x_ref, hb_ref, ma_ref, gb_ref, wb_ref, wo_ref, g_ref, b_ref, o_ref):
    hb = jnp.concatenate([hb_ref[h] for h in range(N_HEADS)], axis=-1)
    m = ma_ref[...] + gb_ref[...] * _dot(hb, wb_ref[...])
    y = _dot(m.astype(_BF16), wo_ref[...])
    o_ref[...] = _layernorm(ALPHA * x_ref[...] + y, g_ref[...], b_ref[...])


def _mixer_out(x, hb, ma, gb, w_b, w_out, g, b, seq):
    t = x.shape[0]
    tm = min(MIXER_OUT_TILE, seq)
    per_seq = seq // tm
    row = pl.BlockSpec((tm, D_MODEL), lambda i: (i, 0))
    head = pl.BlockSpec((None, N_HEADS, tm, HEAD_WIDTH), lambda i: (i // per_seq, 0, i % per_seq, 0))
    return pl.pallas_call(
        _mixer_out_kernel,
        grid=(t // tm,),
        in_specs=[row, head, row, row, _resident((D_MODEL, D_MODEL)), _resident((D_MODEL, D_MODEL)),
                  _resident((1, D_MODEL)), _resident((1, D_MODEL))],
        out_specs=row,
        out_shape=jax.ShapeDtypeStruct((t, D_MODEL), _F32),
        compiler_params=_params("parallel"),
        name="mixer_out",
    )(x, hb, ma, gb, w_b, w_out, g, b)


def _rope_tables(seq):
    half = HEAD_DIM // 2
    inv_freq = ROPE_THETA ** (-jnp.arange(0, HEAD_DIM, 2, dtype=_F32) / HEAD_DIM)
    ang = jnp.arange(seq, dtype=_F32)[:, None] * inv_freq[None, :]
    cos, sin = jnp.cos(ang), jnp.sin(ang)
    zeros = jnp.zeros_like(sin)
    reps = HEAD_WIDTH // HEAD_DIM
    cos_t = jnp.tile(jnp.concatenate([cos, cos], axis=-1), (1, reps))
    sin_a = jnp.tile(jnp.concatenate([-sin, zeros], axis=-1), (1, reps))
    sin_b = jnp.tile(jnp.concatenate([zeros, sin], axis=-1), (1, reps))
    del half
    return cos_t, sin_a, sin_b


def kernel(x, w_in, gate_b, sgu_ln_g, sgu_ln_b, sgu_w, sgu_b, lam, diff_ln_g, w_branch, w_out,
           ffn_w1, ffn_w3, ffn_w2, ln_g, ln_b):
    bsz, seq, _ = x.shape
    cos, sin_a, sin_b = _rope_tables(seq)
    h = x.reshape(bsz * seq, D_MODEL)
    bf = lambda w: w.astype(_BF16)
    for l in range(DEPTH):
        vec = lambda a: a.reshape(1, -1)
        h = _ffn_ln(h, bf(ffn_w1[l, 0]), bf(ffn_w3[l, 0]), bf(ffn_w2[l, 0]),
                    vec(ln_g[l, 0]), vec(ln_b[l, 0]))
        q, k, v, ma, gb = _mixer_in(
            h, bf(w_in[l]), gate_b[l], vec(sgu_ln_g[l]), vec(sgu_ln_b[l]), sgu_w[l],
            sgu_b[l].reshape(N_SGU_GROUPS, CHUNK, 1), bf(w_branch[l, 0]), cos, sin_a, sin_b, bsz, seq)
        hb = _diff_attn(q, k, v, lam[l], vec(diff_ln_g[l]), _lambda_init(l))
        h = _mixer_out(h, hb, ma, gb, bf(w_branch[l, 1]), bf(w_out[l]),
                       vec(ln_g[l, 1]), vec(ln_b[l, 1]), seq)
        h = _ffn_ln(h, bf(ffn_w1[l, 1]), bf(ffn_w3[l, 1]), bf(ffn_w2[l, 1]),
                    vec(ln_g[l, 2]), vec(ln_b[l, 2]))
    return h.reshape(bsz, seq, D_MODEL)
```

```python
import functools
import math

import jax
import jax.numpy as jnp
from jax import lax
from jax.experimental import pallas as pl
from jax.experimental.pallas import tpu as pltpu

D_MODEL = 1024
DEPTH = 4
CHUNK = 128
N_SGU_GROUPS = 8
SGU_GROUP = D_MODEL // N_SGU_GROUPS
HEAD_DIM = 64
N_HEADS = D_MODEL // (2 * HEAD_DIM)
HEAD_WIDTH = 2 * HEAD_DIM
D_FF = 2816
ROPE_THETA = 10000.0
ALPHA = (2.0 * DEPTH) ** 0.25
LN_EPS = 1e-5

OFF_U, OFF_V, OFF_Q, OFF_K, OFF_VAL, OFF_GA, OFF_GB = (
    0, D_MODEL, 2 * D_MODEL, 3 * D_MODEL, 4 * D_MODEL, 5 * D_MODEL, 6 * D_MODEL)
IN_WIDTH = 7 * D_MODEL

VMEM_LIMIT_BYTES = 56 * 1024 * 1024
LANES = 128
BF16_SUBLANES = 16
MASK_VALUE = -1e30
ONES_ROWS = BF16_SUBLANES
V_ROWS = HEAD_WIDTH + ONES_ROWS

FFN_TILE = 1024
MIXER_TILE = 512
MIXER_OUT_TILE = 1024
ATTN_Q_BLOCKS = 1
ATTN_UNROLL = 8
FFN_CHUNKS = ((0, 1024), (1024, 2048), (2048, D_FF))

_BF16 = jnp.bfloat16
_F32 = jnp.float32


def _lambda_init(layer):
    return 0.8 - 0.6 * math.exp(-0.3 * layer)


def _layernorm(y, g, b):
    mu = jnp.mean(y, axis=-1, keepdims=True)
    d = y - mu
    var = jnp.mean(d * d, axis=-1, keepdims=True)
    return d * lax.rsqrt(var + LN_EPS) * g + b


def _gelu(x):
    return 0.5 * x * (1.0 + lax.erf(x * (2.0 ** -0.5)))


def _dot(a, b):
    return jnp.dot(a, b, preferred_element_type=_F32)


def _resident(shape):
    return pl.BlockSpec(shape, lambda *_: (0,) * len(shape), pipeline_mode=pl.Buffered(1))


def _params(*semantics):
    return pltpu.CompilerParams(dimension_semantics=semantics, vmem_limit_bytes=VMEM_LIMIT_BYTES)


def _ffn_ln_kernel(x_ref, w1_ref, w3_ref, w2_ref, g_ref, b_ref, o_ref):
    x = x_ref[...]
    xb = x.astype(_BF16)
    h = None
    for lo, hi in FFN_CHUNKS:
        gate = _dot(xb, w1_ref[:, lo:hi])
        up = _dot(xb, w3_ref[:, lo:hi])
        act = (gate * jax.nn.sigmoid(gate) * up).astype(_BF16)
        part = _dot(act, w2_ref[lo:hi, :])
        h = part if h is None else h + part
    o_ref[...] = _layernorm(ALPHA * x + 0.5 * h, g_ref[...], b_ref[...])


def _ffn_ln(x, w1, w3, w2, g, b):
    t = x.shape[0]
    tm = min(FFN_TILE, t)
    row = pl.BlockSpec((tm, D_MODEL), lambda i: (i, 0))
    return pl.pallas_call(
        _ffn_ln_kernel,
        grid=(t // tm,),
        in_specs=[row, _resident((D_MODEL, D_FF)), _resident((D_MODEL, D_FF)),
                  _resident((D_FF, D_MODEL)), _resident((1, D_MODEL)), _resident((1, D_MODEL))],
        out_specs=row,
        out_shape=jax.ShapeDtypeStruct((t, D_MODEL), _F32),
        compiler_params=_params("parallel"),
        name="ffn_ln",
    )(x, w1, w3, w2, g, b)


def _mixer_in_kernel(x_ref, w_ref, gate_b_ref, sg_ref, sb_ref, sw_ref, sbias_ref, wa_ref,
                     cos_ref, sin_a_ref, sin_b_ref,
                     q_ref, k_ref, v_ref, ma_ref, gb_ref, ha_ref):
    tm = x_ref.shape[0]
    xb = x_ref[...].astype(_BF16)

    def proj(off, width=D_MODEL):
        return _dot(xb, w_ref[:, off:off + width])

    u = _gelu(proj(OFF_U))
    v = _layernorm(_gelu(proj(OFF_V)), sg_ref[...], sb_ref[...]).astype(_BF16)
    row = lax.broadcasted_iota(jnp.int32, (CHUNK, CHUNK), 0)
    col = lax.broadcasted_iota(jnp.int32, (CHUNK, CHUNK), 1)
    causal = col <= row
    for g in range(N_SGU_GROUPS):
        w_g = jnp.where(causal, sw_ref[g], 0.0).astype(_BF16)
        bias_g = sbias_ref[g]
        cols = slice(g * SGU_GROUP, (g + 1) * SGU_GROUP)
        for c in range(tm // CHUNK):
            rows = slice(c * CHUNK, (c + 1) * CHUNK)
            s = _dot(w_g, v[rows, cols]) + bias_g
            ha_ref[rows, cols] = (u[rows, cols] * s).astype(_BF16)
    gate_a = jax.nn.sigmoid(proj(OFF_GA) + gate_b_ref[0:1, :])
    ma_ref[...] = (gate_a * _dot(ha_ref[...], wa_ref[...])).astype(ma_ref.dtype)
    gb_ref[...] = jax.nn.sigmoid(proj(OFF_GB) + gate_b_ref[1:2, :]).astype(gb_ref.dtype)

    cos, sin_a, sin_b = cos_ref[...], sin_a_ref[...], sin_b_ref[...]

    def rope(z):
        return (z * cos + pltpu.roll(z, LANES - HEAD_DIM // 2, axis=1) * sin_a
                + pltpu.roll(z, HEAD_DIM // 2, axis=1) * sin_b)

    zq = proj(OFF_Q)
    zk = proj(OFF_K)
    zv = proj(OFF_VAL)
    scale = HEAD_DIM ** -0.5 * math.log2(math.e)
    ones = jnp.ones((ONES_ROWS, tm), _BF16)
    for h in range(N_HEADS):
        cols = slice(h * HEAD_WIDTH, (h + 1) * HEAD_WIDTH)
        q_ref[h] = (rope(zq[:, cols]) * scale).T.astype(_BF16)
        k_ref[h] = rope(zk[:, cols]).astype(_BF16)
        v_ref[h, :HEAD_WIDTH, :] = zv[:, cols].T.astype(_BF16)
        v_ref[h, HEAD_WIDTH:, :] = ones


def _mixer_in(x, w_in, gate_b, sgu_g, sgu_b, sgu_w, sgu_bias, w_a, cos, sin_a, sin_b, bsz, seq):
    t = x.shape[0]
    tm = min(MIXER_TILE, seq)
    per_seq = seq // tm
    row = pl.BlockSpec((tm, D_MODEL), lambda i: (i, 0))
    pos = pl.BlockSpec((tm, HEAD_WIDTH), lambda i: (i % per_seq, 0))
    head = pl.BlockSpec((None, N_HEADS, tm, HEAD_WIDTH), lambda i: (i // per_seq, 0, i % per_seq, 0))
    head_shape = jax.ShapeDtypeStruct((bsz, N_HEADS, seq, HEAD_WIDTH), _BF16)
    def head_t(rows):
        return (pl.BlockSpec((None, N_HEADS, None, rows, tm), lambda i: (i // per_seq, 0, i % per_seq, 0, 0)),
                jax.ShapeDtypeStruct((bsz, N_HEADS, per_seq, rows, tm), _BF16))

    (q_spec, q_shape), (v_spec, v_shape) = head_t(HEAD_WIDTH), head_t(V_ROWS)
    return pl.pallas_call(
        _mixer_in_kernel,
        grid=(t // tm,),
        in_specs=[row, _resident((D_MODEL, IN_WIDTH)), _resident((2, D_MODEL)),
                  _resident((1, D_MODEL)), _resident((1, D_MODEL)),
                  _resident((N_SGU_GROUPS, CHUNK, CHUNK)), _resident((N_SGU_GROUPS, CHUNK, 1)),
                  _resident((D_MODEL, D_MODEL)), pos, pos, pos],
        out_specs=[q_spec, head, v_spec, row, row],
        out_shape=[q_shape, head_shape, v_shape,
                   jax.ShapeDtypeStruct((t, D_MODEL), _BF16), jax.ShapeDtypeStruct((t, D_MODEL), _BF16)],
        scratch_shapes=[pltpu.VMEM((tm, D_MODEL), _BF16)],
        compiler_params=_params("parallel"),
        name="mixer_in",
    )(x, w_in, gate_b, sgu_g, sgu_b, sgu_w, sgu_bias, w_a, cos, sin_a, sin_b)


def _diff_attn_kernel(lam_ref, g_ref, q_ref, k_ref, v_ref, o_ref,
                      qz1, qz2, s1a, s1b, s2a, s2b, x1a, x1b, x2a, x2b, acc1, acc2, m1, m2,
                      *, lam_init):
    tk = k_ref.shape[0] // v_ref.shape[0]
    tq = ATTN_Q_BLOCKS * tk
    n_tiles = q_ref.shape[0] // ATTN_Q_BLOCKS
    state = ((acc1, m1), (acc2, m2))
    slot_a, slot_b = ((s1a, x1a), (s2a, x2a)), ((s1b, x1b), (s2b, x2b))
    slots = (slot_a, slot_b)

    def load_queries(qi):
        qt = jnp.concatenate([q_ref[ATTN_Q_BLOCKS * qi + r] for r in range(ATTN_Q_BLOCKS)], axis=1)
        feat = lax.broadcasted_iota(jnp.int32, qt.shape, 0)
        zero = jnp.zeros_like(qt)
        qz1[...] = jnp.where(feat < HEAD_DIM, qt, zero)
        qz2[...] = jnp.where(feat >= HEAD_DIM, qt, zero)

    def init_state():
        for acc, m in state:
            acc[...] = jnp.zeros_like(acc)
            m[...] = jnp.full_like(m, MASK_VALUE)

    def store_tile(tile):
        def normalized(acc):
            return acc[:HEAD_WIDTH, :] / acc[HEAD_WIDTH:HEAD_WIDTH + 1, :]

        lam = lam_ref[...]
        lam_full = (jnp.exp(jnp.sum(lam[0:1] * lam[1:2], keepdims=True))
                    - jnp.exp(jnp.sum(lam[2:3] * lam[3:4], keepdims=True)) + lam_init)
        o = (normalized(acc1) - lam_full * normalized(acc2)).T
        o = o * lax.rsqrt(jnp.mean(o * o, axis=-1, keepdims=True) + LN_EPS) * g_ref[...]
        rows = pl.ds(pl.multiple_of(tile * tq, tq), tq)
        o_ref[rows, :] = (o * (1.0 - lam_init)).astype(o_ref.dtype)

    def scores(j, slot, diag=None):
        kb = k_ref[pl.ds(pl.multiple_of(j * tk, tk), tk), :]
        for qz, (s_buf, max_buf) in zip((qz1, qz2), slot):
            s = _dot(kb, qz[...])
            if diag is not None:
                key = lax.broadcasted_iota(jnp.int32, s.shape, 0) + diag * tk
                query = lax.broadcasted_iota(jnp.int32, s.shape, 1)
                s = jnp.where(key <= query, s, MASK_VALUE)
            s_buf[...] = s
            max_buf[...] = jnp.max(s, axis=0, keepdims=True)

    def consume(j, slot):
        vb = v_ref[j]
        for (s_buf, max_buf), (acc, m) in zip(slot, state):
            m_old = m[...]
            m_new = jnp.maximum(m_old, max_buf[...])
            alpha = jnp.exp2(m_old - m_new)
            p = jnp.exp2(s_buf[...] - m_new)
            acc[...] = alpha * acc[...] + _dot(vb, p.astype(_BF16))
            m[...] = m_new

    def run(first, count, unmasked, last_scores=True):
        for i in range(count):
            nxt = i + 1
            if nxt < count or last_scores:
                scores(first + nxt, slots[nxt % 2], diag=None if nxt < unmasked else nxt - unmasked)
            consume(first + i, slots[i % 2])

    def trip(i, carry):
        run(ATTN_UNROLL * i, ATTN_UNROLL, ATTN_UNROLL + 1)
        return carry

    def tile(qi, carry):
        first_diag = ATTN_Q_BLOCKS * qi

        @pl.when(qi == 0)
        def _():
            load_queries(qi)
            scores(0, slot_a, diag=0)
            init_state()

        @pl.when(qi > 0)
        def _():
            load_queries(qi)
            scores(0, slot_a)
            store_tile(qi - 1)
            init_state()

        n_trips = jnp.maximum(first_diag - 1, 0) // ATTN_UNROLL
        lax.fori_loop(0, n_trips, trip, 0)
        done = ATTN_UNROLL * n_trips
        size = ATTN_UNROLL // 2
        while size >= 2:
            fits = first_diag - done > size

            @pl.when(fits)
            def _(done=done, size=size):
                run(done, size, size + 1)

            done = done + jnp.where(fits, size, 0)
            size //= 2

        left = first_diag - done
        for n_left in range(3):
            @pl.when(left == n_left)
            def _(n_left=n_left):
                run(first_diag - n_left, n_left + ATTN_Q_BLOCKS, n_left, last_scores=False)

        return carry

    lax.fori_loop(0, n_tiles, tile, 0)
    store_tile(n_tiles - 1)


def _diff_attn(q_t, k, v_t, lam, g, lam_init):
    bsz, _, seq, _ = k.shape
    tk = q_t.shape[-1]
    tq = ATTN_Q_BLOCKS * tk
    assert seq % tq == 0
    q_spec = pl.BlockSpec((None, None, seq // tk, HEAD_WIDTH, tk), lambda b, h: (b, h, 0, 0, 0))
    k_spec = pl.BlockSpec((None, None, seq, HEAD_WIDTH), lambda b, h: (b, h, 0, 0))
    v_spec = pl.BlockSpec((None, None, seq // tk, V_ROWS, tk), lambda b, h: (b, h, 0, 0, 0))
    o_spec = pl.BlockSpec((None, None, seq, HEAD_WIDTH), lambda b, h: (b, h, 0, 0))
    q_map = pltpu.VMEM((HEAD_WIDTH, tq), _BF16)
    score = pltpu.VMEM((tk, tq), _F32)
    stat = pltpu.VMEM((1, tq), _F32)
    acc = pltpu.VMEM((V_ROWS, tq), _F32)
    return pl.pallas_call(
        functools.partial(_diff_attn_kernel, lam_init=lam_init),
        grid=(bsz, N_HEADS),
        in_specs=[_resident((4, HEAD_DIM)), _resident((1, HEAD_WIDTH)), q_spec, k_spec, v_spec],
        out_specs=o_spec,
        out_shape=jax.ShapeDtypeStruct(k.shape, _BF16),
        scratch_shapes=[q_map] * 2 + [score] * 4 + [stat] * 4 + [acc, acc, stat, stat],
        compiler_params=_params("parallel", "parallel"),
        name="diff_attn",
    )(lam, g, q_t, k, v_t)


def _mixer_out_kernel(x_ref, hb_ref, ma_ref, gb_ref, wb_ref, wo_ref, g_ref, b_ref, o_ref):
    hb = jnp.concatenate([hb_ref[h] for h in range(N_HEADS)], axis=-1)
    m = ma_ref[...] + gb_ref[...] * _dot(hb, wb_ref[...])
    y = _dot(m.astype(_BF16), wo_ref[...])
    o_ref[...] = _layernorm(ALPHA * x_ref[...] + y, g_ref[...], b_ref[...])


def _mixer_out(x, hb, ma, gb, w_b, w_out, g, b, seq):
    t = x.shape[0]
    tm = min(MIXER_OUT_TILE, seq)
    assert seq % tm == 0
    per_seq = seq // tm
    row = pl.BlockSpec((tm, D_MODEL), lambda i: (i, 0))
    head = pl.BlockSpec((None, N_HEADS, tm, HEAD_WIDTH), lambda i: (i // per_seq, 0, i % per_seq, 0))
    return pl.pallas_call(
        _mixer_out_kernel,
        grid=(t // tm,),
        in_specs=[row, head, row, row, _resident((D_MODEL, D_MODEL)), _resident((D_MODEL, D_MODEL)),
                  _resident((1, D_MODEL)), _resident((1, D_MODEL))],
        out_specs=row,
        out_shape=jax.ShapeDtypeStruct((t, D_MODEL), _F32),
        compiler_params=_params("parallel"),
        name="mixer_out",
    )(x, hb, ma, gb, w_b, w_out, g, b)


def _rope_tables(seq):
    half = HEAD_DIM // 2
    inv_freq = ROPE_THETA ** (-jnp.arange(0, HEAD_DIM, 2, dtype=_F32) / HEAD_DIM)
    ang = jnp.arange(seq, dtype=_F32)[:, None] * inv_freq[None, :]
    cos, sin = jnp.cos(ang), jnp.sin(ang)
    zeros = jnp.zeros_like(sin)
    reps = HEAD_WIDTH // HEAD_DIM
    cos_t = jnp.tile(jnp.concatenate([cos, cos], axis=-1), (1, reps))
    sin_a = jnp.tile(jnp.concatenate([-sin, zeros], axis=-1), (1, reps))
    sin_b = jnp.tile(jnp.concatenate([zeros, sin], axis=-1), (1, reps))
    del half
    return cos_t, sin_a, sin_b


def kernel(x, w_in, gate_b, sgu_ln_g, sgu_ln_b, sgu_w, sgu_b, lam, diff_ln_g, w_branch, w_out,
           ffn_w1, ffn_w3, ffn_w2, ln_g, ln_b):
    bsz, seq, _ = x.shape
    cos, sin_a, sin_b = _rope_tables(seq)
    h = x.reshape(bsz * seq, D_MODEL)
    bf = lambda w: w.astype(_BF16)
    for l in range(DEPTH):
        vec = lambda a: a.reshape(1, -1)
        h = _ffn_ln(h, bf(ffn_w1[l, 0]), bf(ffn_w3[l, 0]), bf(ffn_w2[l, 0]),
                    vec(ln_g[l, 0]), vec(ln_b[l, 0]))
        q, k, v, ma, gb = _mixer_in(
            h, bf(w_in[l]), gate_b[l], vec(sgu_ln_g[l]), vec(sgu_ln_b[l]), sgu_w[l],
            sgu_b[l].reshape(N_SGU_GROUPS, CHUNK, 1), bf(w_branch[l, 0]), cos, sin_a, sin_b, bsz, seq)
        hb = _diff_attn(q, k, v, lam[l], vec(diff_ln_g[l]), _lambda_init(l))
        h = _mixer_out(h, hb, ma, gb, bf(w_branch[l, 1]), bf(w_out[l]),
                       vec(ln_g[l, 1]), vec(ln_b[l, 1]), seq)
        h = _ffn_ln(h, bf(ffn_w1[l, 1]), bf(ffn_w3[l, 1]), bf(ffn_w2[l, 1]),
                    vec(ln_g[l, 2]), vec(ln_b[l, 2]))
    return h.reshape(bsz, seq, D_MODEL)
```

```python
import functools
import math

import jax
import jax.numpy as jnp
from jax import lax
from jax.experimental import pallas as pl
from jax.experimental.pallas import tpu as pltpu

D_MODEL = 1024
DEPTH = 4
CHUNK = 128
N_SGU_GROUPS = 8
SGU_GROUP = D_MODEL // N_SGU_GROUPS
HEAD_DIM = 64
N_HEADS = D_MODEL // (2 * HEAD_DIM)
HEAD_WIDTH = 2 * HEAD_DIM
D_FF = 2816
ROPE_THETA = 10000.0
ALPHA = (2.0 * DEPTH) ** 0.25
LN_EPS = 1e-5

OFF_U, OFF_V, OFF_Q, OFF_K, OFF_VAL, OFF_GA, OFF_GB = (
    0, D_MODEL, 2 * D_MODEL, 3 * D_MODEL, 4 * D_MODEL, 5 * D_MODEL, 6 * D_MODEL)
IN_WIDTH = 7 * D_MODEL

VMEM_LIMIT_BYTES = 56 * 1024 * 1024
LANES = 128
BF16_SUBLANES = 16
MASK_VALUE = -1e30
ONES_ROWS = BF16_SUBLANES
V_ROWS = HEAD_WIDTH + ONES_ROWS

FFN_TILE = 1024
MIXER_TILE = 512
MIXER_OUT_TILE = 1024
ATTN_Q_BLOCKS = 1
ATTN_UNROLL = 8
FFN_CHUNKS = ((0, 1024), (1024, 2048), (2048, D_FF))

_BF16 = jnp.bfloat16
_F32 = jnp.float32


def _lambda_init(layer):
    return 0.8 - 0.6 * math.exp(-0.3 * layer)


def _layernorm(y, g, b):
    mu = jnp.mean(y, axis=-1, keepdims=True)
    d = y - mu
    var = jnp.mean(d * d, axis=-1, keepdims=True)
    return d * lax.rsqrt(var + LN_EPS) * g + b


def _gelu(x):
    return 0.5 * x * (1.0 + lax.erf(x * (2.0 ** -0.5)))


def _dot(a, b):
    return jnp.dot(a, b, preferred_element_type=_F32)


def _resident(shape, lead=()):
    return pl.BlockSpec((None,) * len(lead) + tuple(shape),
                        lambda *_: tuple(lead) + (0,) * len(shape), pipeline_mode=pl.Buffered(1))


def _params(*semantics):
    return pltpu.CompilerParams(dimension_semantics=semantics, vmem_limit_bytes=VMEM_LIMIT_BYTES)


def _ffn_ln_kernel(x_ref, w1_ref, w3_ref, w2_ref, g_ref, b_ref, o_ref):
    x = x_ref[...]
    xb = x.astype(_BF16)
    h = None
    for lo, hi in FFN_CHUNKS:
        gate = _dot(xb, w1_ref[:, lo:hi])
        up = _dot(xb, w3_ref[:, lo:hi])
        act = (gate * jax.nn.sigmoid(gate) * up).astype(_BF16)
        part = _dot(act, w2_ref[lo:hi, :])
        h = part if h is None else h + part
    o_ref[...] = _layernorm(ALPHA * x + 0.5 * h, g_ref[...], b_ref[...])


def _ffn_ln(x, w1, w3, w2, which, g, b):
    t = x.shape[0]
    tm = min(FFN_TILE, t)
    row = pl.BlockSpec((tm, D_MODEL), lambda i: (i, 0))
    return pl.pallas_call(
        _ffn_ln_kernel,
        grid=(t // tm,),
        in_specs=[row, _resident((D_MODEL, D_FF), which), _resident((D_MODEL, D_FF), which),
                  _resident((D_FF, D_MODEL), which), _resident((1, D_MODEL)), _resident((1, D_MODEL))],
        out_specs=row,
        out_shape=jax.ShapeDtypeStruct((t, D_MODEL), _F32),
        compiler_params=_params("parallel"),
        name="ffn_ln",
    )(x, w1, w3, w2, g, b)


def _mixer_in_kernel(x_ref, w_ref, gate_b_ref, sg_ref, sb_ref, sw_ref, sbias_ref, wa_ref,
                     cos_ref, sin_a_ref, sin_b_ref,
                     q_ref, k_ref, v_ref, ma_ref, gb_ref, ha_ref):
    tm = x_ref.shape[0]
    xb = x_ref[...].astype(_BF16)

    def proj(off, width=D_MODEL):
        return _dot(xb, w_ref[:, off:off + width])

    v = _layernorm(_gelu(proj(OFF_V)), sg_ref[...], sb_ref[...]).astype(_BF16)

    cos, sin_a, sin_b = cos_ref[...], sin_a_ref[...], sin_b_ref[...]

    def rope(z):
        return (z * cos + pltpu.roll(z, LANES - HEAD_DIM // 2, axis=1) * sin_a
                + pltpu.roll(z, HEAD_DIM // 2, axis=1) * sin_b)

    zq = proj(OFF_Q)
    zk = proj(OFF_K)
    zv = proj(OFF_VAL)
    scale = HEAD_DIM ** -0.5 * math.log2(math.e)
    ones = jnp.ones((ONES_ROWS, tm), _BF16)
    for h in range(N_HEADS):
        cols = slice(h * HEAD_WIDTH, (h + 1) * HEAD_WIDTH)
        q_ref[h] = (rope(zq[:, cols]) * scale).T.astype(_BF16)
        k_ref[h] = rope(zk[:, cols]).astype(_BF16)
        v_ref[h, :HEAD_WIDTH, :] = zv[:, cols].T.astype(_BF16)
        v_ref[h, HEAD_WIDTH:, :] = ones

    gb_ref[...] = jax.nn.sigmoid(proj(OFF_GB) + gate_b_ref[1:2, :]).astype(gb_ref.dtype)
    gate_a = jax.nn.sigmoid(proj(OFF_GA) + gate_b_ref[0:1, :])

    u = _gelu(proj(OFF_U))
    row = lax.broadcasted_iota(jnp.int32, (CHUNK, CHUNK), 0)
    col = lax.broadcasted_iota(jnp.int32, (CHUNK, CHUNK), 1)
    causal = col <= row
    for g in range(N_SGU_GROUPS):
        w_g = jnp.where(causal, sw_ref[g], 0.0).astype(_BF16)
        bias_g = sbias_ref[g]
        cols = slice(g * SGU_GROUP, (g + 1) * SGU_GROUP)
        for c in range(tm // CHUNK):
            rows = slice(c * CHUNK, (c + 1) * CHUNK)
            s = _dot(w_g, v[rows, cols]) + bias_g
            ha_ref[rows, cols] = (u[rows, cols] * s).astype(_BF16)
    ma_ref[...] = (gate_a * _dot(ha_ref[...], wa_ref[...])).astype(ma_ref.dtype)


def _mixer_in(x, layer, w_in, gate_b, sgu_g, sgu_b, sgu_w, sgu_bias, w_branch, cos, sin_a, sin_b,
              bsz, seq):
    t = x.shape[0]
    tm = min(MIXER_TILE, seq)
    per_seq = seq // tm
    row = pl.BlockSpec((tm, D_MODEL), lambda i: (i, 0))
    pos = pl.BlockSpec((tm, HEAD_WIDTH), lambda i: (i % per_seq, 0))
    head = pl.BlockSpec((None, N_HEADS, tm, HEAD_WIDTH), lambda i: (i // per_seq, 0, i % per_seq, 0))
    head_shape = jax.ShapeDtypeStruct((bsz, N_HEADS, seq, HEAD_WIDTH), _BF16)
    def head_t(rows):
        return (pl.BlockSpec((None, N_HEADS, None, rows, tm), lambda i: (i // per_seq, 0, i % per_seq, 0, 0)),
                jax.ShapeDtypeStruct((bsz, N_HEADS, per_seq, rows, tm), _BF16))

    (q_spec, q_shape), (v_spec, v_shape) = head_t(HEAD_WIDTH), head_t(V_ROWS)
    return pl.pallas_call(
        _mixer_in_kernel,
        grid=(t // tm,),
        in_specs=[row, _resident((D_MODEL, IN_WIDTH), (layer,)), _resident((2, D_MODEL)),
                  _resident((1, D_MODEL)), _resident((1, D_MODEL)),
                  _resident((N_SGU_GROUPS, CHUNK, CHUNK)), _resident((N_SGU_GROUPS, CHUNK, 1)),
                  _resident((D_MODEL, D_MODEL), (layer, 0)), pos, pos, pos],
        out_specs=[q_spec, head, v_spec, row, row],
        out_shape=[q_shape, head_shape, v_shape,
                   jax.ShapeDtypeStruct((t, D_MODEL), _BF16), jax.ShapeDtypeStruct((t, D_MODEL), _BF16)],
        scratch_shapes=[pltpu.VMEM((tm, D_MODEL), _BF16)],
        compiler_params=_params("parallel"),
        name="mixer_in",
    )(x, w_in, gate_b, sgu_g, sgu_b, sgu_w, sgu_bias, w_branch, cos, sin_a, sin_b)


def _diff_attn_kernel(lam_ref, g_ref, q_ref, k_ref, v_ref, o_ref,
                      qz1, qz2, s1a, s1b, s2a, s2b, x1a, x1b, x2a, x2b, acc1, acc2, m1, m2,
                      *, lam_init):
    tk = k_ref.shape[0] // v_ref.shape[0]
    tq = ATTN_Q_BLOCKS * tk
    n_tiles = q_ref.shape[0] // ATTN_Q_BLOCKS
    state = ((acc1, m1), (acc2, m2))
    slot_a, slot_b = ((s1a, x1a), (s2a, x2a)), ((s1b, x1b), (s2b, x2b))
    slots = (slot_a, slot_b)

    def load_queries(qi):
        qt = jnp.concatenate([q_ref[ATTN_Q_BLOCKS * qi + r] for r in range(ATTN_Q_BLOCKS)], axis=1)
        feat = lax.broadcasted_iota(jnp.int32, qt.shape, 0)
        zero = jnp.zeros_like(qt)
        qz1[...] = jnp.where(feat < HEAD_DIM, qt, zero)
        qz2[...] = jnp.where(feat >= HEAD_DIM, qt, zero)

    def init_state():
        for acc, m in state:
            acc[...] = jnp.zeros_like(acc)
            m[...] = jnp.full_like(m, MASK_VALUE)

    def store_tile(tile):
        def normalized(acc):
            return acc[:HEAD_WIDTH, :] / acc[HEAD_WIDTH:HEAD_WIDTH + 1, :]

        lam = lam_ref[...]
        lam_full = (jnp.exp(jnp.sum(lam[0:1] * lam[1:2], keepdims=True))
                    - jnp.exp(jnp.sum(lam[2:3] * lam[3:4], keepdims=True)) + lam_init)
        o = (normalized(acc1) - lam_full * normalized(acc2)).T
        o = o * lax.rsqrt(jnp.mean(o * o, axis=-1, keepdims=True) + LN_EPS) * g_ref[...]
        rows = pl.ds(pl.multiple_of(tile * tq, tq), tq)
        o_ref[rows, :] = (o * (1.0 - lam_init)).astype(o_ref.dtype)

    def scores(j, slot, diag=None, maps=(0, 1)):
        kb = k_ref[pl.ds(pl.multiple_of(j * tk, tk), tk), :]
        for mp in maps:
            qz, (s_buf, max_buf) = (qz1, qz2)[mp], slot[mp]
            s = _dot(kb, qz[...])
            if diag is not None:
                key = lax.broadcasted_iota(jnp.int32, s.shape, 0) + diag * tk
                query = lax.broadcasted_iota(jnp.int32, s.shape, 1)
                s = jnp.where(key <= query, s, MASK_VALUE)
            s_buf[...] = s
            max_buf[...] = jnp.max(s, axis=0, keepdims=True)

    def consume(j, slot, maps=(0, 1)):
        vb = v_ref[j]
        for mp in maps:
            (s_buf, max_buf), (acc, m) = slot[mp], state[mp]
            m_old = m[...]
            m_new = jnp.maximum(m_old, max_buf[...])
            alpha = jnp.exp2(m_old - m_new)
            p = jnp.exp2(s_buf[...] - m_new)
            acc[...] = alpha * acc[...] + _dot(vb, p.astype(_BF16))
            m[...] = m_new

    def run(first, count, unmasked, last_scores=True):
        for i in range(count):
            nxt = i + 1
            for mp in (0, 1):
                if nxt < count or last_scores:
                    scores(first + nxt, slots[nxt % 2], diag=None if nxt < unmasked else nxt - unmasked,
                           maps=(mp,))
                consume(first + i, slots[i % 2], maps=(mp,))

    def trip(i, carry):
        run(ATTN_UNROLL * i, ATTN_UNROLL, ATTN_UNROLL + 1)
        return carry

    def tile(qi, carry):
        first_diag = ATTN_Q_BLOCKS * qi

        @pl.when(qi == 0)
        def _():
            load_queries(qi)
            scores(0, slot_a, diag=0)
            init_state()

        @pl.when(qi > 0)
        def _():
            load_queries(qi)
            scores(0, slot_a)
            store_tile(qi - 1)
            init_state()

        n_trips = jnp.maximum(first_diag - 1, 0) // ATTN_UNROLL
        lax.fori_loop(0, n_trips, trip, 0)
        done = ATTN_UNROLL * n_trips
        size = ATTN_UNROLL // 2
        while size >= 2:
            fits = first_diag - done > size

            @pl.when(fits)
            def _(done=done, size=size):
                run(done, size, size + 1)

            done = done + jnp.where(fits, size, 0)
            size //= 2

        left = first_diag - done
        for n_left in range(3):
            @pl.when(left == n_left)
            def _(n_left=n_left):
                run(first_diag - n_left, n_left + ATTN_Q_BLOCKS, n_left, last_scores=False)

        return carry

    lax.fori_loop(0, n_tiles, tile, 0)
    store_tile(n_tiles - 1)


def _diff_attn(q_t, k, v_t, lam, g, lam_init):
    bsz, _, seq, _ = k.shape
    tk = q_t.shape[-1]
    tq = ATTN_Q_BLOCKS * tk
    assert seq % tq == 0
    q_spec = pl.BlockSpec((None, None, seq // tk, HEAD_WIDTH, tk), lambda b, h: (b, h, 0, 0, 0))
    k_spec = pl.BlockSpec((None, None, seq, HEAD_WIDTH), lambda b, h: (b, h, 0, 0))
    v_spec = pl.BlockSpec((None, None, seq // tk, V_ROWS, tk), lambda b, h: (b, h, 0, 0, 0))
    o_spec = pl.BlockSpec((None, None, seq, HEAD_WIDTH), lambda b, h: (b, h, 0, 0))
    q_map = pltpu.VMEM((HEAD_WIDTH, tq), _BF16)
    score = pltpu.VMEM((tk, tq), _F32)
    stat = pltpu.VMEM((1, tq), _F32)
    acc = pltpu.VMEM((V_ROWS, tq), _F32)
    return pl.pallas_call(
        functools.partial(_diff_attn_kernel, lam_init=lam_init),
        grid=(bsz, N_HEADS),
        in_specs=[_resident((4, HEAD_DIM)), _resident((1, HEAD_WIDTH)), q_spec, k_spec, v_spec],
        out_specs=o_spec,
        out_shape=jax.ShapeDtypeStruct(k.shape, _BF16),
        scratch_shapes=[q_map] * 2 + [score] * 4 + [stat] * 4 + [acc, acc, stat, stat],
        compiler_params=_params("parallel", "parallel"),
        name="diff_attn",
    )(lam, g, q_t, k, v_t)


def _mixer_out_kernel(x_ref, hb_ref, ma_ref, gb_ref, wb_ref, wo_ref, g_ref, b_ref, o_ref):
    hb = jnp.concatenate([hb_ref[h] for h in range(N_HEADS)], axis=-1)
    m = ma_ref[...] + gb_ref[...] * _dot(hb, wb_ref[...])
    y = _dot(m.astype(_BF16), wo_ref[...])
    o_ref[...] = _layernorm(ALPHA * x_ref[...] + y, g_ref[...], b_ref[...])


def _mixer_out(x, layer, hb, ma, gb, w_branch, w_out, g, b, seq):
    t = x.shape[0]
    tm = min(MIXER_OUT_TILE, seq)
    assert seq % tm == 0
    per_seq = seq // tm
    row = pl.BlockSpec((tm, D_MODEL), lambda i: (i, 0))
    head = pl.BlockSpec((None, N_HEADS, tm, HEAD_WIDTH), lambda i: (i // per_seq, 0, i % per_seq, 0))
    return pl.pallas_call(
        _mixer_out_kernel,
        grid=(t // tm,),
        in_specs=[row, head, row, row, _resident((D_MODEL, D_MODEL), (layer, 1)),
                  _resident((D_MODEL, D_MODEL), (layer,)),
                  _resident((1, D_MODEL)), _resident((1, D_MODEL))],
        out_specs=row,
        out_shape=jax.ShapeDtypeStruct((t, D_MODEL), _F32),
        compiler_params=_params("parallel"),
        name="mixer_out",
    )(x, hb, ma, gb, w_branch, w_out, g, b)


def _rope_tables(seq):
    half = HEAD_DIM // 2
    inv_freq = ROPE_THETA ** (-jnp.arange(0, HEAD_DIM, 2, dtype=_F32) / HEAD_DIM)
    ang = jnp.arange(seq, dtype=_F32)[:, None] * inv_freq[None, :]
    cos, sin = jnp.cos(ang), jnp.sin(ang)
    zeros = jnp.zeros_like(sin)
    reps = HEAD_WIDTH // HEAD_DIM
    cos_t = jnp.tile(jnp.concatenate([cos, cos], axis=-1), (1, reps))
    sin_a = jnp.tile(jnp.concatenate([-sin, zeros], axis=-1), (1, reps))
    sin_b = jnp.tile(jnp.concatenate([zeros, sin], axis=-1), (1, reps))
    del half
    return cos_t, sin_a, sin_b


def kernel(x, w_in, gate_b, sgu_ln_g, sgu_ln_b, sgu_w, sgu_b, lam, diff_ln_g, w_branch, w_out,
           ffn_w1, ffn_w3, ffn_w2, ln_g, ln_b):
    bsz, seq, _ = x.shape
    cos, sin_a, sin_b = _rope_tables(seq)
    h = x.reshape(bsz * seq, D_MODEL)
    w_in, w_branch, w_out, ffn_w1, ffn_w3, ffn_w2 = (
        w.astype(_BF16) for w in (w_in, w_branch, w_out, ffn_w1, ffn_w3, ffn_w2))
    vec = lambda a: a.reshape(1, -1)
    for l in range(DEPTH):
        h = _ffn_ln(h, ffn_w1, ffn_w3, ffn_w2, (l, 0), vec(ln_g[l, 0]), vec(ln_b[l, 0]))
        q, k, v, ma, gb = _mixer_in(
            h, l, w_in, gate_b[l], vec(sgu_ln_g[l]), vec(sgu_ln_b[l]), sgu_w[l],
            sgu_b[l].reshape(N_SGU_GROUPS, CHUNK, 1), w_branch, cos, sin_a, sin_b, bsz, seq)
        hb = _diff_attn(q, k, v, lam[l], vec(diff_ln_g[l]), _lambda_init(l))
        h = _mixer_out(h, l, hb, ma, gb, w_branch, w_out, vec(ln_g[l, 1]), vec(ln_b[l, 1]), seq)
        h = _ffn_ln(h, ffn_w1, ffn_w3, ffn_w2, (l, 1), vec(ln_g[l, 2]), vec(ln_b[l, 2]))
    return h.reshape(bsz, seq, D_MODEL)
```

```python
import functools
import math

import jax
import jax.numpy as jnp
from jax import lax
from jax.experimental import pallas as pl
from jax.experimental.pallas import tpu as pltpu

D_MODEL = 1024
DEPTH = 4
CHUNK = 128
N_SGU_GROUPS = 8
SGU_GROUP = D_MODEL // N_SGU_GROUPS
HEAD_DIM = 64
N_HEADS = D_MODEL // (2 * HEAD_DIM)
HEAD_WIDTH = 2 * HEAD_DIM
D_FF = 2816
ROPE_THETA = 10000.0
ALPHA = (2.0 * DEPTH) ** 0.25
LN_EPS = 1e-5

OFF_U, OFF_V, OFF_Q, OFF_K, OFF_VAL, OFF_GA, OFF_GB = (
    0, D_MODEL, 2 * D_MODEL, 3 * D_MODEL, 4 * D_MODEL, 5 * D_MODEL, 6 * D_MODEL)
IN_WIDTH = 7 * D_MODEL

VMEM_LIMIT_BYTES = 56 * 1024 * 1024
LANES = 128
BF16_SUBLANES = 16
MASK_VALUE = -1e30
ONES_ROWS = BF16_SUBLANES
V_ROWS = HEAD_WIDTH + ONES_ROWS

FFN_TILE = 1024
MIXER_TILE = 512
MIXER_OUT_TILE = 1024
ATTN_Q_BLOCKS = 1
ATTN_UNROLL = 8
ATTN_TAIL_MAX = 4
FFN_CHUNKS = ((0, 1024), (1024, 2048), (2048, D_FF))

_BF16 = jnp.bfloat16
_F32 = jnp.float32


def _lambda_init(layer):
    return 0.8 - 0.6 * math.exp(-0.3 * layer)


def _layernorm(y, g, b):
    mu = jnp.mean(y, axis=-1, keepdims=True)
    d = y - mu
    var = jnp.mean(d * d, axis=-1, keepdims=True)
    return d * lax.rsqrt(var + LN_EPS) * g + b


def _gelu(x):
    return 0.5 * x * (1.0 + lax.erf(x * (2.0 ** -0.5)))


def _dot(a, b):
    return jnp.dot(a, b, preferred_element_type=_F32)


def _resident(shape, lead=()):
    return pl.BlockSpec((None,) * len(lead) + tuple(shape),
                        lambda *_: tuple(lead) + (0,) * len(shape), pipeline_mode=pl.Buffered(1))


def _params(*semantics):
    return pltpu.CompilerParams(dimension_semantics=semantics, vmem_limit_bytes=VMEM_LIMIT_BYTES)


def _ffn_ln_kernel(x_ref, w1_ref, w3_ref, w2_ref, g_ref, b_ref, o_ref):
    x = x_ref[...]
    xb = x.astype(_BF16)
    h = None
    for lo, hi in FFN_CHUNKS:
        gate = _dot(xb, w1_ref[:, lo:hi])
        up = _dot(xb, w3_ref[:, lo:hi])
        act = (gate * jax.nn.sigmoid(gate) * up).astype(_BF16)
        part = _dot(act, w2_ref[lo:hi, :])
        h = part if h is None else h + part
    o_ref[...] = _layernorm(ALPHA * x + 0.5 * h, g_ref[...], b_ref[...])


def _ffn_ln(x, w1, w3, w2, which, g, b):
    t = x.shape[0]
    tm = min(FFN_TILE, t)
    row = pl.BlockSpec((tm, D_MODEL), lambda i: (i, 0))
    return pl.pallas_call(
        _ffn_ln_kernel,
        grid=(t // tm,),
        in_specs=[row, _resident((D_MODEL, D_FF), which), _resident((D_MODEL, D_FF), which),
                  _resident((D_FF, D_MODEL), which), _resident((1, D_MODEL)), _resident((1, D_MODEL))],
        out_specs=row,
        out_shape=jax.ShapeDtypeStruct((t, D_MODEL), _F32),
        compiler_params=_params("parallel"),
        name="ffn_ln",
    )(x, w1, w3, w2, g, b)


def _mixer_in_kernel(x_ref, w_ref, gate_b_ref, sg_ref, sb_ref, sw_ref, sbias_ref, wa_ref,
                     cos_ref, sin_a_ref, sin_b_ref,
                     q_ref, k_ref, v_ref, ma_ref, gb_ref, ha_ref):
    tm = x_ref.shape[0]
    xb = x_ref[...].astype(_BF16)

    def proj(off, width=D_MODEL):
        return _dot(xb, w_ref[:, off:off + width])

    v = _layernorm(_gelu(proj(OFF_V)), sg_ref[...], sb_ref[...]).astype(_BF16)

    cos, sin_a, sin_b = cos_ref[...], sin_a_ref[...], sin_b_ref[...]

    def rope(z):
        return (z * cos + pltpu.roll(z, LANES - HEAD_DIM // 2, axis=1) * sin_a
                + pltpu.roll(z, HEAD_DIM // 2, axis=1) * sin_b)

    zq = proj(OFF_Q)
    zk = proj(OFF_K)
    zv = proj(OFF_VAL)
    scale = HEAD_DIM ** -0.5 * math.log2(math.e)
    ones = jnp.ones((ONES_ROWS, tm), _BF16)
    for h in range(N_HEADS):
        cols = slice(h * HEAD_WIDTH, (h + 1) * HEAD_WIDTH)
        q_ref[h] = (rope(zq[:, cols]) * scale).T.astype(_BF16)
        k_ref[h] = rope(zk[:, cols]).astype(_BF16)
        v_ref[h, :HEAD_WIDTH, :] = zv[:, cols].T.astype(_BF16)
        v_ref[h, HEAD_WIDTH:, :] = ones

    gb_ref[...] = jax.nn.sigmoid(proj(OFF_GB) + gate_b_ref[1:2, :]).astype(gb_ref.dtype)
    gate_a = jax.nn.sigmoid(proj(OFF_GA) + gate_b_ref[0:1, :])

    u = _gelu(proj(OFF_U))
    row = lax.broadcasted_iota(jnp.int32, (CHUNK, CHUNK), 0)
    col = lax.broadcasted_iota(jnp.int32, (CHUNK, CHUNK), 1)
    causal = col <= row
    for g in range(N_SGU_GROUPS):
        w_g = jnp.where(causal, sw_ref[g], 0.0).astype(_BF16)
        bias_g = sbias_ref[g]
        cols = slice(g * SGU_GROUP, (g + 1) * SGU_GROUP)
        for c in range(tm // CHUNK):
            rows = slice(c * CHUNK, (c + 1) * CHUNK)
            s = _dot(w_g, v[rows, cols]) + bias_g
            ha_ref[rows, cols] = (u[rows, cols] * s).astype(_BF16)
    ma_ref[...] = (gate_a * _dot(ha_ref[...], wa_ref[...])).astype(ma_ref.dtype)


def _mixer_in(x, layer, w_in, gate_b, sgu_g, sgu_b, sgu_w, sgu_bias, w_branch, cos, sin_a, sin_b,
              bsz, seq):
    t = x.shape[0]
    tm = min(MIXER_TILE, seq)
    per_seq = seq // tm
    row = pl.BlockSpec((tm, D_MODEL), lambda i: (i, 0))
    pos = pl.BlockSpec((tm, HEAD_WIDTH), lambda i: (i % per_seq, 0))
    head = pl.BlockSpec((None, N_HEADS, tm, HEAD_WIDTH), lambda i: (i // per_seq, 0, i % per_seq, 0))
    head_shape = jax.ShapeDtypeStruct((bsz, N_HEADS, seq, HEAD_WIDTH), _BF16)
    def head_t(rows):
        return (pl.BlockSpec((None, N_HEADS, None, rows, tm), lambda i: (i // per_seq, 0, i % per_seq, 0, 0)),
                jax.ShapeDtypeStruct((bsz, N_HEADS, per_seq, rows, tm), _BF16))

    (q_spec, q_shape), (v_spec, v_shape) = head_t(HEAD_WIDTH), head_t(V_ROWS)
    return pl.pallas_call(
        _mixer_in_kernel,
        grid=(t // tm,),
        in_specs=[row, _resident((D_MODEL, IN_WIDTH), (layer,)), _resident((2, D_MODEL)),
                  _resident((1, D_MODEL)), _resident((1, D_MODEL)),
                  _resident((N_SGU_GROUPS, CHUNK, CHUNK)), _resident((N_SGU_GROUPS, CHUNK, 1)),
                  _resident((D_MODEL, D_MODEL), (layer, 0)), pos, pos, pos],
        out_specs=[q_spec, head, v_spec, row, row],
        out_shape=[q_shape, head_shape, v_shape,
                   jax.ShapeDtypeStruct((t, D_MODEL), _BF16), jax.ShapeDtypeStruct((t, D_MODEL), _BF16)],
        scratch_shapes=[pltpu.VMEM((tm, D_MODEL), _BF16)],
        compiler_params=_params("parallel"),
        name="mixer_in",
    )(x, w_in, gate_b, sgu_g, sgu_b, sgu_w, sgu_bias, w_branch, cos, sin_a, sin_b)


def _diff_attn_kernel(lam_ref, g_ref, q_ref, k_ref, v_ref, o_ref,
                      qz1, qz2, s1a, s1b, s2a, s2b, x1a, x1b, x2a, x2b, acc1, acc2, m1, m2,
                      *, lam_init):
    tk = k_ref.shape[0] // v_ref.shape[0]
    tq = ATTN_Q_BLOCKS * tk
    n_tiles = q_ref.shape[0] // ATTN_Q_BLOCKS
    state = ((acc1, m1), (acc2, m2))
    slot_a, slot_b = ((s1a, x1a), (s2a, x2a)), ((s1b, x1b), (s2b, x2b))
    slots = (slot_a, slot_b)

    def load_queries(qi):
        qt = jnp.concatenate([q_ref[ATTN_Q_BLOCKS * qi + r] for r in range(ATTN_Q_BLOCKS)], axis=1)
        feat = lax.broadcasted_iota(jnp.int32, qt.shape, 0)
        zero = jnp.zeros_like(qt)
        qz1[...] = jnp.where(feat < HEAD_DIM, qt, zero)
        qz2[...] = jnp.where(feat >= HEAD_DIM, qt, zero)

    def init_state():
        for acc, m in state:
            acc[...] = jnp.zeros_like(acc)
            m[...] = jnp.full_like(m, MASK_VALUE)

    def store_tile(tile):
        def normalized(acc):
            return acc[:HEAD_WIDTH, :] / acc[HEAD_WIDTH:HEAD_WIDTH + 1, :]

        lam = lam_ref[...]
        lam_full = (jnp.exp(jnp.sum(lam[0:1] * lam[1:2], keepdims=True))
                    - jnp.exp(jnp.sum(lam[2:3] * lam[3:4], keepdims=True)) + lam_init)
        o = (normalized(acc1) - lam_full * normalized(acc2)).T
        o = o * lax.rsqrt(jnp.mean(o * o, axis=-1, keepdims=True) + LN_EPS) * g_ref[...]
        rows = pl.ds(pl.multiple_of(tile * tq, tq), tq)
        o_ref[rows, :] = (o * (1.0 - lam_init)).astype(o_ref.dtype)

    def scores(j, slot, diag=None, maps=(0, 1)):
        kb = k_ref[pl.ds(pl.multiple_of(j * tk, tk), tk), :]
        for mp in maps:
            qz, (s_buf, max_buf) = (qz1, qz2)[mp], slot[mp]
            s = _dot(kb, qz[...])
            if diag is not None:
                key = lax.broadcasted_iota(jnp.int32, s.shape, 0) + diag * tk
                query = lax.broadcasted_iota(jnp.int32, s.shape, 1)
                s = jnp.where(key <= query, s, MASK_VALUE)
            s_buf[...] = s
            max_buf[...] = jnp.max(s, axis=0, keepdims=True)

    def consume(j, slot, maps=(0, 1)):
        vb = v_ref[j]
        for mp in maps:
            (s_buf, max_buf), (acc, m) = slot[mp], state[mp]
            m_old = m[...]
            m_new = jnp.maximum(m_old, max_buf[...])
            alpha = jnp.exp2(m_old - m_new)
            p = jnp.exp2(s_buf[...] - m_new)
            acc[...] = alpha * acc[...] + _dot(vb, p.astype(_BF16))
            m[...] = m_new

    def run(first, count, unmasked, last_scores=True):
        for i in range(count):
            nxt = i + 1
            for mp in (0, 1):
                if nxt < count or last_scores:
                    scores(first + nxt, slots[nxt % 2], diag=None if nxt < unmasked else nxt - unmasked,
                           maps=(mp,))
                consume(first + i, slots[i % 2], maps=(mp,))

    def trip(i, carry):
        run(ATTN_UNROLL * i, ATTN_UNROLL, ATTN_UNROLL + 1)
        return carry

    def tile(qi, carry):
        first_diag = ATTN_Q_BLOCKS * qi

        @pl.when(qi == 0)
        def _():
            load_queries(qi)
            scores(0, slot_a, diag=0)
            init_state()

        @pl.when(qi > 0)
        def _():
            load_queries(qi)
            scores(0, slot_a)
            store_tile(qi - 1)
            init_state()

        n_trips = jnp.maximum(first_diag - 1, 0) // ATTN_UNROLL
        lax.fori_loop(0, n_trips, trip, 0)
        done = ATTN_UNROLL * n_trips
        size = ATTN_UNROLL // 2
        while size >= ATTN_TAIL_MAX:
            fits = first_diag - done > size

            @pl.when(fits)
            def _(done=done, size=size):
                run(done, size, size + 1)

            done = done + jnp.where(fits, size, 0)
            size //= 2

        left = first_diag - done
        for n_left in range(ATTN_TAIL_MAX + 1):
            @pl.when(left == n_left)
            def _(n_left=n_left):
                run(first_diag - n_left, n_left + ATTN_Q_BLOCKS, n_left, last_scores=False)

        return carry

    lax.fori_loop(0, n_tiles, tile, 0)
    store_tile(n_tiles - 1)


def _diff_attn(q_t, k, v_t, lam, g, lam_init):
    bsz, _, seq, _ = k.shape
    tk = q_t.shape[-1]
    tq = ATTN_Q_BLOCKS * tk
    assert seq % tq == 0
    q_spec = pl.BlockSpec((None, None, seq // tk, HEAD_WIDTH, tk), lambda b, h: (b, h, 0, 0, 0))
    k_spec = pl.BlockSpec((None, None, seq, HEAD_WIDTH), lambda b, h: (b, h, 0, 0))
    v_spec = pl.BlockSpec((None, None, seq // tk, V_ROWS, tk), lambda b, h: (b, h, 0, 0, 0))
    o_spec = pl.BlockSpec((None, None, seq, HEAD_WIDTH), lambda b, h: (b, h, 0, 0))
    q_map = pltpu.VMEM((HEAD_WIDTH, tq), _BF16)
    score = pltpu.VMEM((tk, tq), _F32)
    stat = pltpu.VMEM((1, tq), _F32)
    acc = pltpu.VMEM((V_ROWS, tq), _F32)
    return pl.pallas_call(
        functools.partial(_diff_attn_kernel, lam_init=lam_init),
        grid=(bsz, N_HEADS),
        in_specs=[_resident((4, HEAD_DIM)), _resident((1, HEAD_WIDTH)), q_spec, k_spec, v_spec],
        out_specs=o_spec,
        out_shape=jax.ShapeDtypeStruct(k.shape, _BF16),
        scratch_shapes=[q_map] * 2 + [score] * 4 + [stat] * 4 + [acc, acc, stat, stat],
        compiler_params=_params("parallel", "parallel"),
        name="diff_attn",
    )(lam, g, q_t, k, v_t)


def _mixer_out_kernel(x_ref, hb_ref, ma_ref, gb_ref, wb_ref, wo_ref, g_ref, b_ref, o_ref):
    hb = jnp.concatenate([hb_ref[h] for h in range(N_HEADS)], axis=-1)
    m = ma_ref[...] + gb_ref[...] * _dot(hb, wb_ref[...])
    y = _dot(m.astype(_BF16), wo_ref[...])
    o_ref[...] = _layernorm(ALPHA * x_ref[...] + y, g_ref[...], b_ref[...])


def _mixer_out(x, layer, hb, ma, gb, w_branch, w_out, g, b, seq):
    t = x.shape[0]
    tm = min(MIXER_OUT_TILE, seq)
    assert seq % tm == 0
    per_seq = seq // tm
    row = pl.BlockSpec((tm, D_MODEL), lambda i: (i, 0))
    head = pl.BlockSpec((None, N_HEADS, tm, HEAD_WIDTH), lambda i: (i // per_seq, 0, i % per_seq, 0))
    return pl.pallas_call(
        _mixer_out_kernel,
        grid=(t // tm,),
        in_specs=[row, head, row, row, _resident((D_MODEL, D_MODEL), (layer, 1)),
                  _resident((D_MODEL, D_MODEL), (layer,)),
                  _resident((1, D_MODEL)), _resident((1, D_MODEL))],
        out_specs=row,
        out_shape=jax.ShapeDtypeStruct((t, D_MODEL), _F32),
        compiler_params=_params("parallel"),
        name="mixer_out",
    )(x, hb, ma, gb, w_branch, w_out, g, b)


def _rope_tables(seq):
    half = HEAD_DIM // 2
    inv_freq = ROPE_THETA ** (-jnp.arange(0, HEAD_DIM, 2, dtype=_F32) / HEAD_DIM)
    ang = jnp.arange(seq, dtype=_F32)[:, None] * inv_freq[None, :]
    cos, sin = jnp.cos(ang), jnp.sin(ang)
    zeros = jnp.zeros_like(sin)
    reps = HEAD_WIDTH // HEAD_DIM
    cos_t = jnp.tile(jnp.concatenate([cos, cos], axis=-1), (1, reps))
    sin_a = jnp.tile(jnp.concatenate([-sin, zeros], axis=-1), (1, reps))
    sin_b = jnp.tile(jnp.concatenate([zeros, sin], axis=-1), (1, reps))
    del half
    return cos_t, sin_a, sin_b


def kernel(x, w_in, gate_b, sgu_ln_g, sgu_ln_b, sgu_w, sgu_b, lam, diff_ln_g, w_branch, w_out,
           ffn_w1, ffn_w3, ffn_w2, ln_g, ln_b):
    bsz, seq, _ = x.shape
    cos, sin_a, sin_b = _rope_tables(seq)
    h = x.reshape(bsz * seq, D_MODEL)
    w_in, w_branch, w_out, ffn_w1, ffn_w3, ffn_w2 = (
        w.astype(_BF16) for w in (w_in, w_branch, w_out, ffn_w1, ffn_w3, ffn_w2))
    vec = lambda a: a.reshape(1, -1)
    for l in range(DEPTH):
        h = _ffn_ln(h, ffn_w1, ffn_w3, ffn_w2, (l, 0), vec(ln_g[l, 0]), vec(ln_b[l, 0]))
        q, k, v, ma, gb = _mixer_in(
            h, l, w_in, gate_b[l], vec(sgu_ln_g[l]), vec(sgu_ln_b[l]), sgu_w[l],
            sgu_b[l].reshape(N_SGU_GROUPS, CHUNK, 1), w_branch, cos, sin_a, sin_b, bsz, seq)
        hb = _diff_attn(q, k, v, lam[l], vec(diff_ln_g[l]), _lambda_init(l))
        h = _mixer_out(h, l, hb, ma, gb, w_branch, w_out, vec(ln_g[l, 1]), vec(ln_b[l, 1]), seq)
        h = _ffn_ln(h, ffn_w1, ffn_w3, ffn_w2, (l, 1), vec(ln_g[l, 2]), vec(ln_b[l, 2]))
    return h.reshape(bsz, seq, D_MODEL)
```

```python
import functools
import math

import jax
import jax.numpy as jnp
from jax import lax
from jax.experimental import pallas as pl
from jax.experimental.pallas import tpu as pltpu

D_MODEL = 1024
DEPTH = 4
CHUNK = 128
N_SGU_GROUPS = 8
SGU_GROUP = D_MODEL // N_SGU_GROUPS
HEAD_DIM = 64
N_HEADS = D_MODEL // (2 * HEAD_DIM)
HEAD_WIDTH = 2 * HEAD_DIM
D_FF = 2816
ROPE_THETA = 10000.0
ALPHA = (2.0 * DEPTH) ** 0.25
LN_EPS = 1e-5

OFF_U, OFF_V, OFF_Q, OFF_K, OFF_VAL, OFF_GA, OFF_GB = (
    0, D_MODEL, 2 * D_MODEL, 3 * D_MODEL, 4 * D_MODEL, 5 * D_MODEL, 6 * D_MODEL)
IN_WIDTH = 7 * D_MODEL

VMEM_LIMIT_BYTES = 56 * 1024 * 1024
LANES = 128
BF16_SUBLANES = 16
MASK_VALUE = -1e30
ONES_ROWS = BF16_SUBLANES
V_ROWS = HEAD_WIDTH + ONES_ROWS

FFN_TILE = 1024
MIXER_TILE = 512
MIXER_OUT_TILE = 1024
ATTN_Q_BLOCKS = 1
ATTN_UNROLL = 8
FFN_CHUNKS = ((0, 1024), (1024, 2048), (2048, D_FF))

_BF16 = jnp.bfloat16
_F32 = jnp.float32


def _lambda_init(layer):
    return 0.8 - 0.6 * math.exp(-0.3 * layer)


def _layernorm(y, g, b):
    mu = jnp.mean(y, axis=-1, keepdims=True)
    d = y - mu
    var = jnp.mean(d * d, axis=-1, keepdims=True)
    return d * lax.rsqrt(var + LN_EPS) * g + b


def _gelu(x):
    return 0.5 * x * (1.0 + lax.erf(x * (2.0 ** -0.5)))


def _dot(a, b):
    return jnp.dot(a, b, preferred_element_type=_F32)


def _resident(shape, lead=()):
    return pl.BlockSpec((None,) * len(lead) + tuple(shape),
                        lambda *_: tuple(lead) + (0,) * len(shape), pipeline_mode=pl.Buffered(1))


def _params(*semantics):
    return pltpu.CompilerParams(dimension_semantics=semantics, vmem_limit_bytes=VMEM_LIMIT_BYTES)


def _exact_zero_row(v):
    bits = lax.bitcast_convert_type(v, jnp.uint32)
    gone = lax.shift_right_logical(lax.shift_right_logical(bits, jnp.uint32(16)), jnp.uint32(16))
    return jnp.max(gone.astype(jnp.int32), axis=0, keepdims=True).astype(_F32)


def _ffn_ln_kernel(x_ref, w1_ref, w3_ref, w2_ref, g_ref, b_ref, o_ref, y_ref):
    step = pl.program_id(0)
    n_tiles = pl.num_programs(0) - 1

    def store_previous_tile():
        o = _layernorm(y_ref[...], g_ref[...], b_ref[...])
        o_ref[...] = o
        return o

    def residual_plus_ffn(with_previous):
        x = x_ref[...]
        xb = x.astype(_BF16)
        h = None
        for i, (lo, hi) in enumerate(FFN_CHUNKS):
            gate = _dot(xb, w1_ref[:, lo:hi])
            up = _dot(xb, w3_ref[:, lo:hi])
            act = gate * jax.nn.sigmoid(gate) * up
            if with_previous and i == len(FFN_CHUNKS) - 1:
                act = act + _exact_zero_row(store_previous_tile())[:, :hi - lo]
            part = _dot(act.astype(_BF16), w2_ref[lo:hi, :])
            h = part if h is None else h + part
        y_ref[...] = ALPHA * x + 0.5 * h

    @pl.when(step == 0)
    def _():
        residual_plus_ffn(with_previous=False)

    @pl.when((step > 0) & (step < n_tiles))
    def _():
        residual_plus_ffn(with_previous=True)

    @pl.when(step == n_tiles)
    def _():
        store_previous_tile()


def _ffn_ln(x, w1, w3, w2, which, g, b):
    t = x.shape[0]
    tm = min(FFN_TILE, t)
    n_tiles = t // tm
    x_spec = pl.BlockSpec((tm, D_MODEL), lambda i: (jnp.minimum(i, n_tiles - 1), 0))
    o_spec = pl.BlockSpec((tm, D_MODEL), lambda i: (jnp.maximum(i - 1, 0), 0))
    return pl.pallas_call(
        _ffn_ln_kernel,
        grid=(n_tiles + 1,),
        in_specs=[x_spec, _resident((D_MODEL, D_FF), which), _resident((D_MODEL, D_FF), which),
                  _resident((D_FF, D_MODEL), which), _resident((1, D_MODEL)), _resident((1, D_MODEL))],
        out_specs=o_spec,
        out_shape=jax.ShapeDtypeStruct((t, D_MODEL), _F32),
        scratch_shapes=[pltpu.VMEM((tm, D_MODEL), _F32)],
        compiler_params=_params("arbitrary"),
        name="ffn_ln",
    )(x, w1, w3, w2, g, b)


def _mixer_in_kernel(x_ref, w_ref, gate_b_ref, sg_ref, sb_ref, sw_ref, sbias_ref, wa_ref,
                     cos_ref, sin_a_ref, sin_b_ref,
                     q_ref, k_ref, v_ref, ma_ref, gb_ref, ha_ref):
    tm = x_ref.shape[0]
    xb = x_ref[...].astype(_BF16)

    def proj(off, width=D_MODEL):
        return _dot(xb, w_ref[:, off:off + width])

    v = _layernorm(_gelu(proj(OFF_V)), sg_ref[...], sb_ref[...]).astype(_BF16)

    cos, sin_a, sin_b = cos_ref[...], sin_a_ref[...], sin_b_ref[...]

    def rope(z):
        return (z * cos + pltpu.roll(z, LANES - HEAD_DIM // 2, axis=1) * sin_a
                + pltpu.roll(z, HEAD_DIM // 2, axis=1) * sin_b)

    zq = proj(OFF_Q)
    zk = proj(OFF_K)
    zv = proj(OFF_VAL)
    scale = HEAD_DIM ** -0.5 * math.log2(math.e)
    ones = jnp.ones((ONES_ROWS, tm), _BF16)
    for h in range(N_HEADS):
        cols = slice(h * HEAD_WIDTH, (h + 1) * HEAD_WIDTH)
        q_ref[h] = (rope(zq[:, cols]) * scale).T.astype(_BF16)
        k_ref[h] = rope(zk[:, cols]).astype(_BF16)
        v_ref[h, :HEAD_WIDTH, :] = zv[:, cols].T.astype(_BF16)
        v_ref[h, HEAD_WIDTH:, :] = ones

    gb_ref[...] = jax.nn.sigmoid(proj(OFF_GB) + gate_b_ref[1:2, :]).astype(gb_ref.dtype)
    gate_a = jax.nn.sigmoid(proj(OFF_GA) + gate_b_ref[0:1, :])

    u = _gelu(proj(OFF_U))
    row = lax.broadcasted_iota(jnp.int32, (CHUNK, CHUNK), 0)
    col = lax.broadcasted_iota(jnp.int32, (CHUNK, CHUNK), 1)
    causal = col <= row
    for g in range(N_SGU_GROUPS):
        w_g = jnp.where(causal, sw_ref[g], 0.0).astype(_BF16)
        bias_g = sbias_ref[g]
        cols = slice(g * SGU_GROUP, (g + 1) * SGU_GROUP)
        for c in range(tm // CHUNK):
            rows = slice(c * CHUNK, (c + 1) * CHUNK)
            s = _dot(w_g, v[rows, cols]) + bias_g
            ha_ref[rows, cols] = (u[rows, cols] * s).astype(_BF16)
    ma_ref[...] = (gate_a * _dot(ha_ref[...], wa_ref[...])).astype(ma_ref.dtype)


def _mixer_in(x, layer, w_in, gate_b, sgu_g, sgu_b, sgu_w, sgu_bias, w_branch, cos, sin_a, sin_b,
              bsz, seq):
    t = x.shape[0]
    tm = min(MIXER_TILE, seq)
    per_seq = seq // tm
    row = pl.BlockSpec((tm, D_MODEL), lambda i: (i, 0))
    pos = pl.BlockSpec((tm, HEAD_WIDTH), lambda i: (i % per_seq, 0))
    head = pl.BlockSpec((None, N_HEADS, tm, HEAD_WIDTH), lambda i: (i // per_seq, 0, i % per_seq, 0))
    head_shape = jax.ShapeDtypeStruct((bsz, N_HEADS, seq, HEAD_WIDTH), _BF16)
    def head_t(rows):
        return (pl.BlockSpec((None, N_HEADS, None, rows, tm), lambda i: (i // per_seq, 0, i % per_seq, 0, 0)),
                jax.ShapeDtypeStruct((bsz, N_HEADS, per_seq, rows, tm), _BF16))

    (q_spec, q_shape), (v_spec, v_shape) = head_t(HEAD_WIDTH), head_t(V_ROWS)
    return pl.pallas_call(
        _mixer_in_kernel,
        grid=(t // tm,),
        in_specs=[row, _resident((D_MODEL, IN_WIDTH), (layer,)), _resident((2, D_MODEL)),
                  _resident((1, D_MODEL)), _resident((1, D_MODEL)),
                  _resident((N_SGU_GROUPS, CHUNK, CHUNK)), _resident((N_SGU_GROUPS, CHUNK, 1)),
                  _resident((D_MODEL, D_MODEL), (layer, 0)), pos, pos, pos],
        out_specs=[q_spec, head, v_spec, row, row],
        out_shape=[q_shape, head_shape, v_shape,
                   jax.ShapeDtypeStruct((t, D_MODEL), _BF16), jax.ShapeDtypeStruct((t, D_MODEL), _BF16)],
        scratch_shapes=[pltpu.VMEM((tm, D_MODEL), _BF16)],
        compiler_params=_params("parallel"),
        name="mixer_in",
    )(x, w_in, gate_b, sgu_g, sgu_b, sgu_w, sgu_bias, w_branch, cos, sin_a, sin_b)


def _diff_attn_kernel(lam_ref, g_ref, q_ref, k_ref, v_ref, o_ref,
                      qz1, qz2, s1a, s1b, s2a, s2b, x1a, x1b, x2a, x2b, acc1, acc2, m1, m2,
                      *, lam_init):
    tk = k_ref.shape[0] // v_ref.shape[0]
    tq = ATTN_Q_BLOCKS * tk
    n_tiles = q_ref.shape[0] // ATTN_Q_BLOCKS
    state = ((acc1, m1), (acc2, m2))
    slot_a, slot_b = ((s1a, x1a), (s2a, x2a)), ((s1b, x1b), (s2b, x2b))
    slots = (slot_a, slot_b)

    def load_queries(qi):
        qt = jnp.concatenate([q_ref[ATTN_Q_BLOCKS * qi + r] for r in range(ATTN_Q_BLOCKS)], axis=1)
        feat = lax.broadcasted_iota(jnp.int32, qt.shape, 0)
        zero = jnp.zeros_like(qt)
        qz1[...] = jnp.where(feat < HEAD_DIM, qt, zero)
        qz2[...] = jnp.where(feat >= HEAD_DIM, qt, zero)

    def init_state():
        for acc, m in state:
            acc[...] = jnp.zeros_like(acc)
            m[...] = jnp.full_like(m, MASK_VALUE)

    def store_tile(tile):
        def normalized(acc):
            return acc[:HEAD_WIDTH, :] / acc[HEAD_WIDTH:HEAD_WIDTH + 1, :]

        lam = lam_ref[...]
        lam_full = (jnp.exp(jnp.sum(lam[0:1] * lam[1:2], keepdims=True))
                    - jnp.exp(jnp.sum(lam[2:3] * lam[3:4], keepdims=True)) + lam_init)
        o = (normalized(acc1) - lam_full * normalized(acc2)).T
        o = o * lax.rsqrt(jnp.mean(o * o, axis=-1, keepdims=True) + LN_EPS) * g_ref[...]
        rows = pl.ds(pl.multiple_of(tile * tq, tq), tq)
        o_ref[rows, :] = (o * (1.0 - lam_init)).astype(o_ref.dtype)

    def scores(j, slot, diag=None, maps=(0, 1)):
        kb = k_ref[pl.ds(pl.multiple_of(j * tk, tk), tk), :]
        for mp in maps:
            qz, (s_buf, max_buf) = (qz1, qz2)[mp], slot[mp]
            s = _dot(kb, qz[...])
            if diag is not None:
                key = lax.broadcasted_iota(jnp.int32, s.shape, 0) + diag * tk
                query = lax.broadcasted_iota(jnp.int32, s.shape, 1)
                s = jnp.where(key <= query, s, MASK_VALUE)
            s_buf[...] = s
            max_buf[...] = jnp.max(s, axis=0, keepdims=True)

    def consume(j, slot, maps=(0, 1)):
        vb = v_ref[j]
        for mp in maps:
            (s_buf, max_buf), (acc, m) = slot[mp], state[mp]
            m_old = m[...]
            m_new = jnp.maximum(m_old, max_buf[...])
            alpha = jnp.exp2(m_old - m_new)
            p = jnp.exp2(s_buf[...] - m_new)
            acc[...] = alpha * acc[...] + _dot(vb, p.astype(_BF16))
            m[...] = m_new

    def run(first, count, unmasked, last_scores=True):
        for i in range(count):
            nxt = i + 1
            for mp in (0, 1):
                if nxt < count or last_scores:
                    scores(first + nxt, slots[nxt % 2], diag=None if nxt < unmasked else nxt - unmasked,
                           maps=(mp,))
                consume(first + i, slots[i % 2], maps=(mp,))

    def trip(i, carry):
        run(ATTN_UNROLL * i, ATTN_UNROLL, ATTN_UNROLL + 1)
        return carry

    def tile(qi, carry):
        first_diag = ATTN_Q_BLOCKS * qi

        @pl.when(qi == 0)
        def _():
            load_queries(qi)
            scores(0, slot_a, diag=0)
            init_state()

        @pl.when(qi > 0)
        def _():
            load_queries(qi)
            scores(0, slot_a)
            store_tile(qi - 1)
            init_state()

        n_trips = jnp.maximum(first_diag - 1, 0) // ATTN_UNROLL
        lax.fori_loop(0, n_trips, trip, 0)
        done = ATTN_UNROLL * n_trips
        size = ATTN_UNROLL // 2
        while size >= 2:
            fits = first_diag - done > size

            @pl.when(fits)
            def _(done=done, size=size):
                run(done, size, size + 1)

            done = done + jnp.where(fits, size, 0)
            size //= 2

        left = first_diag - done
        for n_left in range(3):
            @pl.when(left == n_left)
            def _(n_left=n_left):
                run(first_diag - n_left, n_left + ATTN_Q_BLOCKS, n_left, last_scores=False)

        return carry

    lax.fori_loop(0, n_tiles, tile, 0)
    store_tile(n_tiles - 1)


def _diff_attn(q_t, k, v_t, lam, g, lam_init):
    bsz, _, seq, _ = k.shape
    tk = q_t.shape[-1]
    tq = ATTN_Q_BLOCKS * tk
    assert seq % tq == 0
    q_spec = pl.BlockSpec((None, None, seq // tk, HEAD_WIDTH, tk), lambda b, h: (b, h, 0, 0, 0))
    k_spec = pl.BlockSpec((None, None, seq, HEAD_WIDTH), lambda b, h: (b, h, 0, 0))
    v_spec = pl.BlockSpec((None, None, seq // tk, V_ROWS, tk), lambda b, h: (b, h, 0, 0, 0))
    o_spec = pl.BlockSpec((None, None, seq, HEAD_WIDTH), lambda b, h: (b, h, 0, 0))
    q_map = pltpu.VMEM((HEAD_WIDTH, tq), _BF16)
    score = pltpu.VMEM((tk, tq), _F32)
    stat = pltpu.VMEM((1, tq), _F32)
    acc = pltpu.VMEM((V_ROWS, tq), _F32)
    return pl.pallas_call(
        functools.partial(_diff_attn_kernel, lam_init=lam_init),
        grid=(bsz, N_HEADS),
        in_specs=[_resident((4, HEAD_DIM)), _resident((1, HEAD_WIDTH)), q_spec, k_spec, v_spec],
        out_specs=o_spec,
        out_shape=jax.ShapeDtypeStruct(k.shape, _BF16),
        scratch_shapes=[q_map] * 2 + [score] * 4 + [stat] * 4 + [acc, acc, stat, stat],
        compiler_params=_params("parallel", "parallel"),
        name="diff_attn",
    )(lam, g, q_t, k, v_t)


def _mixer_out_kernel(x_ref, hb_ref, ma_ref, gb_ref, wb_ref, wo_ref, g_ref, b_ref, o_ref):
    hb = jnp.concatenate([hb_ref[h] for h in range(N_HEADS)], axis=-1)
    m = ma_ref[...] + gb_ref[...] * _dot(hb, wb_ref[...])
    y = _dot(m.astype(_BF16), wo_ref[...])
    o_ref[...] = _layernorm(ALPHA * x_ref[...] + y, g_ref[...], b_ref[...])


def _mixer_out(x, layer, hb, ma, gb, w_branch, w_out, g, b, seq):
    t = x.shape[0]
    tm = min(MIXER_OUT_TILE, seq)
    assert seq % tm == 0
    per_seq = seq // tm
    row = pl.BlockSpec((tm, D_MODEL), lambda i: (i, 0))
    head = pl.BlockSpec((None, N_HEADS, tm, HEAD_WIDTH), lambda i: (i // per_seq, 0, i % per_seq, 0))
    return pl.pallas_call(
        _mixer_out_kernel,
        grid=(t // tm,),
        in_specs=[row, head, row, row, _resident((D_MODEL, D_MODEL), (layer, 1)),
                  _resident((D_MODEL, D_MODEL), (layer,)),
                  _resident((1, D_MODEL)), _resident((1, D_MODEL))],
        out_specs=row,
        out_shape=jax.ShapeDtypeStruct((t, D_MODEL), _F32),
        compiler_params=_params("parallel"),
        name="mixer_out",
    )(x, hb, ma, gb, w_branch, w_out, g, b)


def _rope_tables(seq):
    half = HEAD_DIM // 2
    inv_freq = ROPE_THETA ** (-jnp.arange(0, HEAD_DIM, 2, dtype=_F32) / HEAD_DIM)
    ang = jnp.arange(seq, dtype=_F32)[:, None] * inv_freq[None, :]
    cos, sin = jnp.cos(ang), jnp.sin(ang)
    zeros = jnp.zeros_like(sin)
    reps = HEAD_WIDTH // HEAD_DIM
    cos_t = jnp.tile(jnp.concatenate([cos, cos], axis=-1), (1, reps))
    sin_a = jnp.tile(jnp.concatenate([-sin, zeros], axis=-1), (1, reps))
    sin_b = jnp.tile(jnp.concatenate([zeros, sin], axis=-1), (1, reps))
    del half
    return cos_t, sin_a, sin_b


def kernel(x, w_in, gate_b, sgu_ln_g, sgu_ln_b, sgu_w, sgu_b, lam, diff_ln_g, w_branch, w_out,
           ffn_w1, ffn_w3, ffn_w2, ln_g, ln_b):
    bsz, seq, _ = x.shape
    cos, sin_a, sin_b = _rope_tables(seq)
    h = x.reshape(bsz * seq, D_MODEL)
    w_in, w_branch, w_out, ffn_w1, ffn_w3, ffn_w2 = (
        w.astype(_BF16) for w in (w_in, w_branch, w_out, ffn_w1, ffn_w3, ffn_w2))
    vec = lambda a: a.reshape(1, -1)
    for l in range(DEPTH):
        h = _ffn_ln(h, ffn_w1, ffn_w3, ffn_w2, (l, 0), vec(ln_g[l, 0]), vec(ln_b[l, 0]))
        q, k, v, ma, gb = _mixer_in(
            h, l, w_in, gate_b[l], vec(sgu_ln_g[l]), vec(sgu_ln_b[l]), sgu_w[l],
            sgu_b[l].reshape(N_SGU_GROUPS, CHUNK, 1), w_branch, cos, sin_a, sin_b, bsz, seq)
        hb = _diff_attn(q, k, v, lam[l], vec(diff_ln_g[l]), _lambda_init(l))
        h = _mixer_out(h, l, hb, ma, gb, w_branch, w_out, vec(ln_g[l, 1]), vec(ln_b[l, 1]), seq)
        h = _ffn_ln(h, ffn_w1, ffn_w3, ffn_w2, (l, 1), vec(ln_g[l, 2]), vec(ln_b[l, 2]))
    return h.reshape(bsz, seq, D_MODEL)
```
